```python
import jax, jax.numpy as jnp
from jax import lax
import numpy as np

D_MODEL = 1024
BATCH = 4
SEQ = 4096
DEPTH = 4
DEC_BATCH = 8
DEC_SEQ = 16
PAST_LEN = 4096

CHUNK = 64
ML_HEADS = 4
ML_DH = D_MODEL // ML_HEADS
ML_W = ML_HEADS * ML_DH
CONV_CH = D_MODEL
CONV_W = 31
CONV_BUF = CONV_W - 1
XA_HEADS = 4
XA_DH = D_MODEL // XA_HEADS
XA_W = XA_HEADS * XA_DH
MEM_LEN = 256
N_BRANCH = 3
BRANCH_W = 1024
D_FF = -(-8 * D_MODEL // (3 * 256)) * 256
IN_SIZES = (ML_W, ML_W, ML_W, ML_W, ML_HEADS, ML_HEADS, 2 * CONV_CH, XA_W, N_BRANCH * D_MODEL)
N_IN = sum(IN_SIZES)
IN_SPLITS = tuple(int(s) for s in np.cumsum(IN_SIZES)[:-1])
RMS_EPS = 1e-6
LN_EPS = 1e-5

kernel_name = "hybrid_mlstm_conformer_memxattn_stream_step"


def rms_norm(x, g):
    xf = x.astype(jnp.float32)
    y = xf * lax.rsqrt(jnp.mean(xf * xf, axis=-1, keepdims=True) + RMS_EPS)
    return (y * g.astype(jnp.float32)).astype(x.dtype)


def layer_norm(x, g, b):
    xf = x.astype(jnp.float32)
    mu = jnp.mean(xf, axis=-1, keepdims=True)
    xc = xf - mu
    y = xc * lax.rsqrt(jnp.mean(xc * xc, axis=-1, keepdims=True) + LN_EPS)
    return (y * g.astype(jnp.float32) + b.astype(jnp.float32)).astype(x.dtype)


def _mlstm_block(carry, blk):
    c, n, m = carry
    q, k, v, ig, lf = blk
    L = q.shape[2]
    b = jnp.cumsum(lf, axis=-1)
    causal = jnp.tril(jnp.ones((L, L), dtype=bool))
    d = jnp.where(causal, b[..., :, None] - b[..., None, :] + ig[..., None, :], -jnp.inf)
    a = b + m[..., None]
    m_row = jnp.maximum(a, jnp.max(d, axis=-1))
    s = jnp.einsum('bhjd,bhsd->bhjs', q, k) * jnp.exp(d - m_row[..., None])
    w_inter = jnp.exp(a - m_row)
    num = w_inter[..., None] * jnp.einsum('bhjd,bhed->bhje', q, c) + jnp.einsum('bhjs,bhse->bhje', s, v)
    den = w_inter * jnp.einsum('bhjd,bhd->bhj', q, n) + jnp.sum(s, axis=-1)
    h = num / jnp.maximum(jnp.abs(den), jnp.exp(-m_row))[..., None]
    b_last = b[..., -1]
    g = b_last[..., None] - b + ig
    m_new = jnp.maximum(b_last + m, jnp.max(g, axis=-1))
    w_old = jnp.exp(b_last + m - m_new)
    w_s = jnp.exp(g - m_new[..., None])
    c_new = w_old[..., None, None] * c + jnp.einsum('bhs,bhse,bhsd->bhed', w_s, v, k)
    n_new = w_old[..., None] * n + jnp.einsum('bhs,bhsd->bhd', w_s, k)
    return (c_new, n_new, m_new), h


def mlstm(q, k, v, ig, lf, state, block):
    B, T = q.shape[0], q.shape[1]
    nb = T // block

    def to_blocks(t):
        t = t.reshape((B, nb, block) + t.shape[2:])
        return jnp.swapaxes(jnp.moveaxis(t, 1, 0), 2, 3)

    xs = (to_blocks(q), to_blocks(k), to_blocks(v), to_blocks(ig), to_blocks(lf))
    state, hs = lax.scan(_mlstm_block, state, xs)
    h = jnp.moveaxis(jnp.swapaxes(hs, 2, 3), 0, 1).reshape(B, T, ML_HEADS, ML_DH)
    return h, state


def causal_dwconv(u, buf, w, b):
    ext = jnp.concatenate([buf.astype(u.dtype), u], axis=1)
    y = lax.conv_general_dilated(ext, w[:, None, :].astype(u.dtype), window_strides=(1,), padding='VALID',
                                 dimension_numbers=('NWC', 'WIO', 'NWC'), feature_group_count=u.shape[-1])
    return y + b.astype(u.dtype), ext[:, -CONV_BUF:]


def mem_kv(mem, g, w):
    B = mem.shape[0]
    kv = rms_norm(mem, g) @ w
    mk, mv = jnp.split(kv, 2, axis=-1)
    return mk.reshape(B, MEM_LEN, XA_HEADS, XA_DH), mv.reshape(B, MEM_LEN, XA_HEADS, XA_DH)


def mem_attention(q, mk, mv):
    s = jnp.einsum('bthd,bmhd->bhtm', q, mk.astype(q.dtype)).astype(jnp.float32) * (XA_DH ** -0.5)
    p = jax.nn.softmax(s, axis=-1).astype(q.dtype)
    return jnp.einsum('bhtm,bmhd->bthd', p, mv.astype(q.dtype))


def layer(x, ml_state, conv_buf, mk, mv, g_mix_pre, g_mix_post, g_ffn_pre, g_ffn_post, w_in, b_gate,
          g_mhead, conv_w, conv_b, ln_g, ln_b, w_branch, w_out, w_ffn_in, w_ffn_out):
    f32 = jnp.float32
    B, T, _ = x.shape
    block = min(CHUNK, T)
    xn = rms_norm(x, g_mix_pre)
    q, k, v, o, ig, fg, glu, xq, gates = jnp.split(xn @ w_in, IN_SPLITS, axis=-1)

    heads = lambda t: t.reshape(B, T, ML_HEADS, ML_DH).astype(f32)
    ig_t = ig.astype(f32) + b_gate[:ML_HEADS].astype(f32)
    lf_t = jax.nn.log_sigmoid(fg.astype(f32) + b_gate[ML_HEADS:].astype(f32))
    h, ml_state = mlstm(heads(q), heads(k) * (ML_DH ** -0.5), heads(v), ig_t, lf_t, ml_state, block)
    h = h * lax.rsqrt(jnp.mean(h * h, axis=-1, keepdims=True) + RMS_EPS) * g_mhead.astype(f32).reshape(ML_HEADS, ML_DH)
    y_ml = (h.reshape(B, T, ML_W) * jax.nn.sigmoid(o.astype(f32))).astype(x.dtype)

    ga, gb = jnp.split(glu, 2, axis=-1)
    u = ga * jax.nn.sigmoid(gb)
    c, conv_buf = causal_dwconv(u, conv_buf, conv_w, conv_b)
    y_cv = jax.nn.silu(layer_norm(c, ln_g, ln_b))

    y_xa = mem_attention(xq.reshape(B, T, XA_HEADS, XA_DH), mk, mv).reshape(B, T, XA_W)

    mixed = (jax.nn.sigmoid(gates[..., 0:D_MODEL]) * (y_ml @ w_branch[0])
             + jax.nn.sigmoid(gates[..., D_MODEL:2 * D_MODEL]) * (y_cv @ w_branch[1])
             + jax.nn.sigmoid(gates[..., 2 * D_MODEL:]) * (y_xa @ w_branch[2]))
    x = x + rms_norm(mixed @ w_out, g_mix_post)

    fa, fb = jnp.split(rms_norm(x, g_ffn_pre) @ w_ffn_in, 2, axis=-1)
    x = x + rms_norm((jax.nn.silu(fa) * fb) @ w_ffn_out, g_ffn_post)
    return x, ml_state, conv_buf


def setup_inputs(seed: int = 0) -> dict:
    key = jax.random.key(seed)
    ks = jax.random.split(key, 32)
    nrm = lambda k, shape, s: jax.random.normal(k, shape, jnp.float32) * s
    gain = lambda k, shape: 1.0 + 0.05 * jax.random.normal(k, shape, jnp.float32)
    f_bias = jnp.linspace(3.0, 6.0, ML_HEADS, dtype=jnp.float32)[None, :] + nrm(ks[0], (DEPTH, ML_HEADS), 0.1)
    i_bias = nrm(ks[1], (DEPTH, ML_HEADS), 0.1)
    return {
        "x_prompt": nrm(ks[2], (BATCH, SEQ, D_MODEL), 1.0),
        "x_sample": nrm(ks[3], (DEC_BATCH, DEC_SEQ, D_MODEL), 1.0),
        "mem_prompt": nrm(ks[4], (BATCH, MEM_LEN, D_MODEL), 1.0),
        "state_mlstm_c": nrm(ks[5], (DEPTH, DEC_BATCH, ML_HEADS, ML_DH, ML_DH), ML_DH ** -0.5),
        "state_mlstm_n": nrm(ks[6], (DEPTH, DEC_BATCH, ML_HEADS, ML_DH), 0.1),
        "state_mlstm_m": nrm(ks[7], (DEPTH, DEC_BATCH, ML_HEADS), 0.5),
        "state_conv": nrm(ks[8], (DEPTH, DEC_BATCH, CONV_BUF, CONV_CH), 0.5),
        "cache_mem_k": nrm(ks[9], (DEPTH, DEC_BATCH, MEM_LEN, XA_HEADS, XA_DH), 1.0),
        "cache_mem_v": nrm(ks[10], (DEPTH, DEC_BATCH, MEM_LEN, XA_HEADS, XA_DH), 1.0),
        "norm_mix_pre": gain(ks[11], (DEPTH, D_MODEL)),
        "norm_mix_post": gain(ks[12], (DEPTH, D_MODEL)),
        "norm_ffn_pre": gain(ks[13], (DEPTH, D_MODEL)),
        "norm_ffn_post": gain(ks[14], (DEPTH, D_MODEL)),
        "w_in": nrm(ks[15], (DEPTH, D_MODEL, N_IN), D_MODEL ** -0.5),
        "b_gate": jnp.concatenate([i_bias, f_bias], axis=-1),
        "mlstm_head_norm": gain(ks[16], (DEPTH, ML_W)),
        "conv_w": nrm(ks[17], (DEPTH, CONV_W, CONV_CH), CONV_W ** -0.5),
        "conv_b": nrm(ks[18], (DEPTH, CONV_CH), 0.02),
        "conv_ln_g": gain(ks[19], (DEPTH, CONV_CH)),
        "conv_ln_b": nrm(ks[20], (DEPTH, CONV_CH), 0.02),
        "mem_norm": gain(ks[21], (DEPTH, D_MODEL)),
        "w_mem_kv": nrm(ks[22], (DEPTH, D_MODEL, 2 * XA_W), D_MODEL ** -0.5),
        "w_branch": nrm(ks[23], (DEPTH, N_BRANCH, BRANCH_W, D_MODEL), BRANCH_W ** -0.5),
        "w_out": nrm(ks[24], (DEPTH, D_MODEL, D_MODEL), D_MODEL ** -0.5),
        "w_ffn_in": nrm(ks[25], (DEPTH, D_MODEL, 2 * D_FF), D_MODEL ** -0.5),
        "w_ffn_out": nrm(ks[26], (DEPTH, D_FF, D_MODEL), D_FF ** -0.5),
    }


def reference(x_prompt, x_sample, mem_prompt, state_mlstm_c, state_mlstm_n, state_mlstm_m, state_conv,
              cache_mem_k, cache_mem_v, norm_mix_pre, norm_mix_post, norm_ffn_pre, norm_ffn_post, w_in, b_gate,
              mlstm_head_norm, conv_w, conv_b, conv_ln_g, conv_ln_b, mem_norm, w_mem_kv, w_branch, w_out,
              w_ffn_in, w_ffn_out):
    f32 = jnp.float32
    bp = x_prompt.shape[0]
    yp, ys = x_prompt, x_sample
    pc, pn, pm, pconv, pmk, pmv = [], [], [], [], [], []
    sc, sn, sm, sconv = [], [], [], []
    for l in range(DEPTH):
        p = (norm_mix_pre[l], norm_mix_post[l], norm_ffn_pre[l], norm_ffn_post[l], w_in[l], b_gate[l],
             mlstm_head_norm[l], conv_w[l], conv_b[l], conv_ln_g[l], conv_ln_b[l], w_branch[l], w_out[l],
             w_ffn_in[l], w_ffn_out[l])
        mk_p, mv_p = mem_kv(mem_prompt, mem_norm[l], w_mem_kv[l])
        zero_state = (jnp.zeros((bp, ML_HEADS, ML_DH, ML_DH), f32), jnp.zeros((bp, ML_HEADS, ML_DH), f32),
                      jnp.zeros((bp, ML_HEADS), f32))
        zero_buf = jnp.zeros((bp, CONV_BUF, CONV_CH), x_prompt.dtype)
        yp, (c1, n1, m1), buf1 = layer(yp, zero_state, zero_buf, mk_p, mv_p, *p)
        s_state = (state_mlstm_c[l].astype(f32), state_mlstm_n[l].astype(f32), state_mlstm_m[l].astype(f32))
        ys, (c2, n2, m2), buf2 = layer(ys, s_state, state_conv[l], cache_mem_k[l], cache_mem_v[l], *p)
        pc.append(c1); pn.append(n1); pm.append(m1); pconv.append(buf1); pmk.append(mk_p); pmv.append(mv_p)
        sc.append(c2); sn.append(n2); sm.append(m2); sconv.append(buf2)
    return (yp, ys, jnp.stack(pc), jnp.stack(pn), jnp.stack(pm), jnp.stack(pconv), jnp.stack(pmk), jnp.stack(pmv),
            jnp.stack(sc), jnp.stack(sn), jnp.stack(sm), jnp.stack(sconv))
```

```python
import functools

import jax
import jax.numpy as jnp
from jax import lax
from jax.experimental import pallas as pl
from jax.experimental.pallas import tpu as pltpu

F32 = jnp.float32
BF16 = jnp.bfloat16

D_MODEL = 1024
ML_HEADS = 4
ML_DH = 256
XA_HEADS = 4
XA_DH = 256
MEM_LEN = 256
CONV_W = 31
CONV_BUF = CONV_W - 1
D_FF = 2816
RMS_EPS = 1e-6
LN_EPS = 1e-5

N_MAIN = 10 * D_MODEL
COL_Q, COL_K, COL_V, COL_O, COL_GA, COL_GB, COL_XQ, COL_GATE = 0, 1, 2, 3, 4, 5, 6, 7
GATE_PAD = 128
GATE_ROWS = 16

V7X_VMEM_LIMIT_BYTES = 56 * 1024 * 1024
SUBLANES = 8
LANES = 128
CONV_ROW_BLOCK = 64
CONV_PAD = 32

NT_DIMS = (((1,), (1,)), ((), ()))
TN_DIMS = (((0,), (0,)), ((), ()))


def _resident(block_shape, index_map):
    return pl.BlockSpec(block_shape, index_map, pipeline_mode=pl.Buffered(1))


def _params(semantics):
    return pltpu.CompilerParams(dimension_semantics=semantics, vmem_limit_bytes=V7X_VMEM_LIMIT_BYTES)


def _rms(x, g):
    return x * lax.rsqrt(jnp.mean(x * x, axis=-1, keepdims=True) + RMS_EPS) * g


def _inproj_kernel(x_ref, g_ref, w_ref, wg_ref, wgt_ref, p_ref, go_ref, gto_ref, *, chunk):
    xn = _rms(x_ref[...], g_ref[...]).astype(BF16)
    for j in range(N_MAIN // chunk):
        sl = slice(j * chunk, (j + 1) * chunk)
        p_ref[:, sl] = jnp.dot(xn, w_ref[:, sl], preferred_element_type=F32).astype(BF16)
    go_ref[...] = jnp.dot(xn, wg_ref[...], preferred_element_type=F32)
    gto_ref[...] = lax.dot_general(wgt_ref[...], xn, NT_DIMS, preferred_element_type=F32)


def _inproj(x2d, g, w_main, w_gate, w_gate_t, *, batch, seq, tm):
    m = x2d.shape[0]
    grid = (m // tm,)
    if seq % tm == 0:
        tiles_per_seq = seq // tm
        gt_shape = (batch, GATE_ROWS, seq)
        gt_spec = pl.BlockSpec((None, GATE_ROWS, tm), lambda i: (i // tiles_per_seq, 0, i % tiles_per_seq))
    else:
        assert m == tm
        gt_shape = (1, GATE_ROWS, m)
        gt_spec = pl.BlockSpec((None, GATE_ROWS, tm), lambda i: (0, 0, 0))
    p, go, gto = pl.pallas_call(
        functools.partial(_inproj_kernel, chunk=D_MODEL),
        grid=grid,
        in_specs=[
            pl.BlockSpec((tm, D_MODEL), lambda i: (i, 0)),
            _resident((1, D_MODEL), lambda i: (0, 0)),
            _resident((D_MODEL, N_MAIN), lambda i: (0, 0)),
            _resident((D_MODEL, GATE_PAD), lambda i: (0, 0)),
            _resident((GATE_ROWS, D_MODEL), lambda i: (0, 0)),
        ],
        out_specs=[
            pl.BlockSpec((tm, N_MAIN), lambda i: (i, 0)),
            pl.BlockSpec((tm, GATE_PAD), lambda i: (i, 0)),
            gt_spec,
        ],
        out_shape=[
            jax.ShapeDtypeStruct((m, N_MAIN), BF16),
            jax.ShapeDtypeStruct((m, GATE_PAD), F32),
            jax.ShapeDtypeStruct(gt_shape, F32),
        ],
        compiler_params=_params(("parallel",)),
        name="inproj",
    )(x2d, g, w_main, w_gate, w_gate_t)
    if seq % tm != 0:
        gto = jnp.transpose(gto.reshape(GATE_ROWS, batch, seq), (1, 0, 2))
    return p, go, gto


def _mlstm_kernel(*refs, L, has_state):
    if has_state:
        (q_ref, k_ref, v_ref, o_ref, g_ref, gt_ref, brow_ref, bcol_ref, gh_ref,
         c0_ref, n0_ref, m0_ref, y_ref, c_ref, n_ref, m_ref) = refs
    else:
        (q_ref, k_ref, v_ref, o_ref, g_ref, gt_ref, brow_ref, bcol_ref, gh_ref,
         y_ref, c_ref, n_ref, m_ref) = refs
    t = pl.program_id(1)

    @pl.when(t == 0)
    def _():
        if has_state:
            c_ref[...] = c0_ref[...]
            n_ref[...] = n0_ref[...]
            m_ref[...] = m0_ref[...]
        else:
            c_ref[...] = jnp.zeros_like(c_ref)
            n_ref[...] = jnp.zeros_like(n_ref)
            m_ref[...] = jnp.zeros_like(m_ref)

    row = lax.broadcasted_iota(jnp.int32, (L, L), 0)
    col = lax.broadcasted_iota(jnp.int32, (L, L), 1)
    causal = col <= row
    tril = causal.astype(F32)
    triu = (row <= col).astype(F32)

    gates_c = g_ref[...] + brow_ref[...]
    gates_r = gt_ref[...] + bcol_ref[...]
    cum_c = jnp.dot(tril, jax.nn.log_sigmoid(gates_c), precision=lax.Precision.HIGHEST,
                    preferred_element_type=F32)
    cum_r = jnp.dot(jax.nn.log_sigmoid(gates_r), triu, precision=lax.Precision.HIGHEST,
                    preferred_element_type=F32)

    for h in range(ML_HEADS):
        hs = slice(h * ML_DH, (h + 1) * ML_DH)
        ig_c = gates_c[:, h:h + 1]
        b_c = cum_c[:, ML_HEADS + h:ML_HEADS + h + 1]
        ig_r = gates_r[h:h + 1, :]
        b_r = cum_r[ML_HEADS + h:ML_HEADS + h + 1, :]
        m_prev = m_ref[:, h:h + 1]

        d = jnp.where(causal, b_c - b_r + ig_r, -jnp.inf)
        a = b_c + m_prev
        m_row = jnp.maximum(a, jnp.max(d, axis=1, keepdims=True))

        q = q_ref[:, hs]
        ks = k_ref[:, hs] * jnp.asarray(ML_DH ** -0.5, BF16)
        v = v_ref[:, hs]
        s = lax.dot_general(q, ks, NT_DIMS, preferred_element_type=F32) * jnp.exp(d - m_row)
        w_inter = jnp.exp(a - m_row)

        c = c_ref[h]
        n = n_ref[h:h + 1, :]
        qc = lax.dot_general(q, c.astype(BF16), NT_DIMS, preferred_element_type=F32)
        num = w_inter * qc + jnp.dot(s.astype(BF16), v, preferred_element_type=F32)
        qn = jnp.sum(q.astype(F32) * n, axis=1, keepdims=True)
        den = w_inter * qn + jnp.sum(s, axis=1, keepdims=True)
        hh = num / jnp.maximum(jnp.abs(den), jnp.exp(-m_row))

        b_last = b_c[L - 1:L, :]
        g_c = b_last - b_c + ig_c
        m_new = jnp.maximum(b_last + m_prev, jnp.max(g_c, axis=0, keepdims=True))
        w_old = jnp.exp(b_last + m_prev - m_new)
        w_s = jnp.exp(g_c - m_new)
        wv = (w_s * v.astype(F32)).astype(BF16)
        c_ref[h] = w_old * c + lax.dot_general(wv, ks, TN_DIMS, preferred_element_type=F32)
        n_ref[h:h + 1, :] = w_old * n + jnp.sum(w_s * ks.astype(F32), axis=0, keepdims=True)
        m_ref[:, h:h + 1] = m_new

        hn = hh * lax.rsqrt(jnp.mean(hh * hh, axis=1, keepdims=True) + RMS_EPS) * gh_ref[:, hs]
        y_ref[:, hs] = (hn * jax.nn.sigmoid(o_ref[:, hs].astype(F32))).astype(BF16)


def _mlstm(p, go, gto, bias_row, bias_col, g_head, state, *, batch, seq, L):
    nb = seq // L
    m = batch * seq
    has_state = state is not None
    rows = lambda b, t: b * nb + t
    in_specs = [
        pl.BlockSpec((L, D_MODEL), lambda b, t: (rows(b, t), COL_Q)),
        pl.BlockSpec((L, D_MODEL), lambda b, t: (rows(b, t), COL_K)),
        pl.BlockSpec((L, D_MODEL), lambda b, t: (rows(b, t), COL_V)),
        pl.BlockSpec((L, D_MODEL), lambda b, t: (rows(b, t), COL_O)),
        pl.BlockSpec((L, GATE_PAD), lambda b, t: (rows(b, t), 0)),
        pl.BlockSpec((None, GATE_ROWS, L), lambda b, t: (b, 0, t)),
        pl.BlockSpec((1, GATE_PAD), lambda b, t: (0, 0)),
        pl.BlockSpec((GATE_ROWS, 1), lambda b, t: (0, 0)),
        pl.BlockSpec((1, D_MODEL), lambda b, t: (0, 0)),
    ]
    args = [p, p, p, p, go, gto, bias_row, bias_col, g_head]
    state_specs = [
        pl.BlockSpec((None, ML_HEADS, ML_DH, ML_DH), lambda b, t: (b, 0, 0, 0)),
        pl.BlockSpec((None, ML_HEADS, ML_DH), lambda b, t: (b, 0, 0)),
        pl.BlockSpec((None, 1, ML_HEADS), lambda b, t: (b, 0, 0)),
    ]
    if has_state:
        c0, n0, m0 = state
        in_specs += state_specs
        args += [c0, n0, m0.reshape(batch, 1, ML_HEADS)]
    y, c, n, mm = pl.pallas_call(
        functools.partial(_mlstm_kernel, L=L, has_state=has_state),
        grid=(batch, nb),
        in_specs=in_specs,
        out_specs=[pl.BlockSpec((L, D_MODEL), lambda b, t: (rows(b, t), 0))] + state_specs,
        out_shape=[
            jax.ShapeDtypeStruct((m, D_MODEL), BF16),
            jax.ShapeDtypeStruct((batch, ML_HEADS, ML_DH, ML_DH), F32),
            jax.ShapeDtypeStruct((batch, ML_HEADS, ML_DH), F32),
            jax.ShapeDtypeStruct((batch, 1, ML_HEADS), F32),
        ],
        compiler_params=_params(("parallel", "arbitrary")),
        name="mlstm",
    )(*args)
    return y, c, n, mm.reshape(batch, ML_HEADS)


def _conv_kernel(*refs, tt, has_state):
    if has_state:
        ga_ref, gb_ref, w_ref, cb_ref, lng_ref, lnb_ref, s0_ref, y_ref, so_ref, ext_ref, acc_ref = refs
    else:
        ga_ref, gb_ref, w_ref, cb_ref, lng_ref, lnb_ref, y_ref, so_ref, ext_ref, acc_ref = refs
    t = pl.program_id(1)
    lo = CONV_PAD - CONV_BUF

    @pl.when(t == 0)
    def _():
        if has_state:
            ext_ref[lo:CONV_PAD, :] = s0_ref[...]
        else:
            ext_ref[0:CONV_PAD, :] = jnp.zeros((CONV_PAD, D_MODEL), F32)

    ext_ref[CONV_PAD:CONV_PAD + tt, :] = ga_ref[...].astype(F32) * jax.nn.sigmoid(gb_ref[...].astype(F32))

    rb = min(CONV_ROW_BLOCK, tt)
    for cblk in range(D_MODEL // LANES):
        cs = slice(cblk * LANES, (cblk + 1) * LANES)
        for r in range(tt // rb):
            acc = jnp.zeros((rb, LANES), F32)
            for j in range(CONV_W):
                start = lo + j + r * rb
                acc = acc + ext_ref[start:start + rb, cs] * w_ref[j:j + 1, cs]
            acc_ref[r * rb:(r + 1) * rb, cs] = acc + cb_ref[:, cs]

    tail = ext_ref[tt + lo:tt + CONV_PAD, :]
    so_ref[...] = tail
    ext_ref[lo:CONV_PAD, :] = tail

    c = acc_ref[...]
    mu = jnp.mean(c, axis=-1, keepdims=True)
    xc = c - mu
    yn = xc * lax.rsqrt(jnp.mean(xc * xc, axis=-1, keepdims=True) + LN_EPS) * lng_ref[...] + lnb_ref[...]
    y_ref[...] = (yn * jax.nn.sigmoid(yn)).astype(BF16)


def _conv(p, conv_w, conv_b, ln_g, ln_b, state, *, batch, seq, tt):
    nt = seq // tt
    m = batch * seq
    has_state = state is not None
    rows = lambda b, t: b * nt + t
    vec = lambda: pl.BlockSpec((1, D_MODEL), lambda b, t: (0, 0))
    in_specs = [
        pl.BlockSpec((tt, D_MODEL), lambda b, t: (rows(b, t), COL_GA)),
        pl.BlockSpec((tt, D_MODEL), lambda b, t: (rows(b, t), COL_GB)),
        pl.BlockSpec((CONV_W, D_MODEL), lambda b, t: (0, 0)),
        vec(), vec(), vec(),
    ]
    args = [p, p, conv_w, conv_b, ln_g, ln_b]
    state_spec = pl.BlockSpec((None, CONV_BUF, D_MODEL), lambda b, t: (b, 0, 0))
    if has_state:
        in_specs.append(state_spec)
        args.append(state)
    y, so = pl.pallas_call(
        functools.partial(_conv_kernel, tt=tt, has_state=has_state),
        grid=(batch, nt),
        in_specs=in_specs,
        out_specs=[pl.BlockSpec((tt, D_MODEL), lambda b, t: (rows(b, t), 0)), state_spec],
        out_shape=[
            jax.ShapeDtypeStruct((m, D_MODEL), BF16),
            jax.ShapeDtypeStruct((batch, CONV_BUF, D_MODEL), F32),
        ],
        scratch_shapes=[
            pltpu.VMEM((CONV_PAD + tt, D_MODEL), F32),
            pltpu.VMEM((tt, D_MODEL), F32),
        ],
        compiler_params=_params(("parallel", "arbitrary")),
        name="conv",
    )(*args)
    return y, so


def _xattn_kernel(q_ref, k_ref, v_ref, y_ref):
    s = lax.dot_general(q_ref[...], k_ref[...].astype(BF16), NT_DIMS, preferred_element_type=F32)
    s = s * (XA_DH ** -0.5)
    e = jnp.exp(s - jnp.max(s, axis=-1, keepdims=True))
    p = e / jnp.sum(e, axis=-1, keepdims=True)
    y_ref[...] = jnp.dot(p.astype(BF16), v_ref[...].astype(BF16), preferred_element_type=F32).astype(BF16)


def _xattn(p, mk, mv, *, batch, seq, tt):
    nt = seq // tt
    m = batch * seq
    kv_spec = pl.BlockSpec((None, MEM_LEN, XA_DH), lambda b, t, h: (b, 0, h))
    return pl.pallas_call(
        _xattn_kernel,
        grid=(batch, nt, XA_HEADS),
        in_specs=[
            pl.BlockSpec((tt, XA_DH), lambda b, t, h: (b * nt + t, COL_XQ * XA_HEADS + h)),
            kv_spec, kv_spec,
        ],
        out_specs=pl.BlockSpec((tt, XA_DH), lambda b, t, h: (b * nt + t, h)),
        out_shape=jax.ShapeDtypeStruct((m, D_MODEL), BF16),
        compiler_params=_params(("parallel", "parallel", "parallel")),
        name="xattn",
    )(p, mk, mv)


def _merge_kernel(x_ref, yml_ref, ycv_ref, yxa_ref, g0_ref, g1_ref, g2_ref, wb_ref, wo_ref, gp_ref, o_ref):
    mixed = None
    for i, (y_ref, gate_ref) in enumerate(((yml_ref, g0_ref), (ycv_ref, g1_ref), (yxa_ref, g2_ref))):
        term = jax.nn.sigmoid(gate_ref[...].astype(F32)) * jnp.dot(y_ref[...], wb_ref[i],
                                                                    preferred_element_type=F32)
        mixed = term if mixed is None else mixed + term
    z = jnp.dot(mixed.astype(BF16), wo_ref[...], preferred_element_type=F32)
    o_ref[...] = x_ref[...] + _rms(z, gp_ref[...])


def _merge(x2d, y_ml, y_cv, y_xa, p, w_branch, w_out, g_post, *, tm):
    m = x2d.shape[0]
    tile = lambda c: pl.BlockSpec((tm, D_MODEL), lambda i: (i, c))
    return pl.pallas_call(
        _merge_kernel,
        grid=(m // tm,),
        in_specs=[
            tile(0), tile(0), tile(0), tile(0),
            tile(COL_GATE), tile(COL_GATE + 1), tile(COL_GATE + 2),
            _resident((3, D_MODEL, D_MODEL), lambda i: (0, 0, 0)),
            _resident((D_MODEL, D_MODEL), lambda i: (0, 0)),
            _resident((1, D_MODEL), lambda i: (0, 0)),
        ],
        out_specs=tile(0),
        out_shape=jax.ShapeDtypeStruct((m, D_MODEL), F32),
        compiler_params=_params(("parallel",)),
        name="merge",
    )(x2d, y_ml, y_cv, y_xa, p, p, p, w_branch, w_out, g_post)


FFN_CHUNKS = ((0, 768), (768, 768), (1536, 768), (2304, 512))


def _ffn_kernel(x_ref, gpre_ref, wa_ref, wb_ref, wo_ref, gpost_ref, o_ref, h_ref):
    x = x_ref[...]
    xn = _rms(x, gpre_ref[...]).astype(BF16)
    for start, size in FFN_CHUNKS:
        sl = slice(start, start + size)
        fa = jnp.dot(xn, wa_ref[:, sl], preferred_element_type=F32)
        fb = jnp.dot(xn, wb_ref[:, sl], preferred_element_type=F32)
        h_ref[:, sl] = (fa * jax.nn.sigmoid(fa) * fb).astype(BF16)
    z = jnp.dot(h_ref[...], wo_ref[...], preferred_element_type=F32)
    o_ref[...] = x + _rms(z, gpost_ref[...])


def _ffn(x2d, g_pre, w_a, w_b, w_o, g_post, *, tm):
    m = x2d.shape[0]
    return pl.pallas_call(
        _ffn_kernel,
        grid=(m // tm,),
        in_specs=[
            pl.BlockSpec((tm, D_MODEL), lambda i: (i, 0)),
            _resident((1, D_MODEL), lambda i: (0, 0)),
            _resident((D_MODEL, D_FF), lambda i: (0, 0)),
            _resident((D_MODEL, D_FF), lambda i: (0, 0)),
            _resident((D_FF, D_MODEL), lambda i: (0, 0)),
            _resident((1, D_MODEL), lambda i: (0, 0)),
        ],
        out_specs=pl.BlockSpec((tm, D_MODEL), lambda i: (i, 0)),
        out_shape=jax.ShapeDtypeStruct((m, D_MODEL), F32),
        scratch_shapes=[pltpu.VMEM((tm, D_FF), BF16)],
        compiler_params=_params(("parallel",)),
        name="ffn",
    )(x2d, g_pre, w_a, w_b, w_o, g_post)


def _memkv_kernel(x_ref, g_ref, w_ref, k_ref, v_ref):
    xn = _rms(x_ref[...], g_ref[...]).astype(BF16)
    k_ref[...] = jnp.dot(xn, w_ref[:, :D_MODEL], preferred_element_type=F32)
    v_ref[...] = jnp.dot(xn, w_ref[:, D_MODEL:], preferred_element_type=F32)


def _memkv(mem2d, g, w_kv, *, tm):
    m = mem2d.shape[0]
    out = jax.ShapeDtypeStruct((m, D_MODEL), F32)
    tile = pl.BlockSpec((tm, D_MODEL), lambda i: (i, 0))
    return pl.pallas_call(
        _memkv_kernel,
        grid=(m // tm,),
        in_specs=[tile, _resident((1, D_MODEL), lambda i: (0, 0)),
                  _resident((D_MODEL, 2 * D_MODEL), lambda i: (0, 0))],
        out_specs=[tile, tile],
        out_shape=[out, out],
        compiler_params=_params(("parallel",)),
        name="memkv",
    )(mem2d, g, w_kv)


def _layer(x2d, lw, ml_state, conv_state, mk, mv, *, batch, seq, tm, L, tt_xa):
    p, go, gto = _inproj(x2d, lw["g_mix_pre"], lw["w_main"], lw["w_gate"], lw["w_gate_t"],
                         batch=batch, seq=seq, tm=tm)
    y_ml, c, n, mm = _mlstm(p, go, gto, lw["bias_row"], lw["bias_col"], lw["g_mhead"], ml_state,
                            batch=batch, seq=seq, L=L)
    y_cv, conv_out = _conv(p, lw["conv_w"], lw["conv_b"], lw["ln_g"], lw["ln_b"], conv_state,
                           batch=batch, seq=seq, tt=L)
    y_xa = _xattn(p, mk, mv, batch=batch, seq=seq, tt=tt_xa)
    x2d = _merge(x2d, y_ml, y_cv, y_xa, p, lw["w_branch"], lw["w_out"], lw["g_mix_post"], tm=tm)
    x2d = _ffn(x2d, lw["g_ffn_pre"], lw["w_ffn_a"], lw["w_ffn_b"], lw["w_ffn_o"], lw["g_ffn_post"], tm=tm)
    return x2d, (c, n, mm), conv_out


def kernel(x_prompt, x_sample, mem_prompt, state_mlstm_c, state_mlstm_n, state_mlstm_m, state_conv,
           cache_mem_k, cache_mem_v, norm_mix_pre, norm_mix_post, norm_ffn_pre, norm_ffn_post, w_in, b_gate,
           mlstm_head_norm, conv_w, conv_b, conv_ln_g, conv_ln_b, mem_norm, w_mem_kv, w_branch, w_out,
           w_ffn_in, w_ffn_out):
    depth = w_in.shape[0]
    bp, sp, _ = x_prompt.shape
    bs, ss, _ = x_sample.shape
    gate_lo = 4 * D_MODEL
    gate_hi = gate_lo + 2 * ML_HEADS

    yp = x_prompt.reshape(bp * sp, D_MODEL)
    ys = x_sample.reshape(bs * ss, D_MODEL)
    mem2d = mem_prompt.reshape(bp * MEM_LEN, D_MODEL)
    row = lambda v: v.reshape(1, -1).astype(F32)

    pc, pn, pm, pconv, pmk, pmv = [], [], [], [], [], []
    sc, sn, sm, sconv = [], [], [], []
    for l in range(depth):
        w_l = w_in[l]
        w_gate_cols = w_l[:, gate_lo:gate_hi]
        lw = {
            "w_main": jnp.concatenate([w_l[:, :gate_lo], w_l[:, gate_hi:]], axis=1).astype(BF16),
            "w_gate": jnp.pad(w_gate_cols, ((0, 0), (0, GATE_PAD - 2 * ML_HEADS))).astype(BF16),
            "w_gate_t": jnp.pad(w_gate_cols.T, ((0, GATE_ROWS - 2 * ML_HEADS), (0, 0))).astype(BF16),
            "bias_row": jnp.pad(b_gate[l], (0, GATE_PAD - 2 * ML_HEADS)).reshape(1, GATE_PAD).astype(F32),
            "bias_col": jnp.pad(b_gate[l], (0, GATE_ROWS - 2 * ML_HEADS)).reshape(GATE_ROWS, 1).astype(F32),
            "g_mix_pre": row(norm_mix_pre[l]), "g_mix_post": row(norm_mix_post[l]),
            "g_ffn_pre": row(norm_ffn_pre[l]), "g_ffn_post": row(norm_ffn_post[l]),
            "g_mhead": row(mlstm_head_norm[l]),
            "conv_w": conv_w[l].astype(F32), "conv_b": row(conv_b[l]),
            "ln_g": row(conv_ln_g[l]), "ln_b": row(conv_ln_b[l]),
            "w_branch": w_branch[l].astype(BF16), "w_out": w_out[l].astype(BF16),
            "w_ffn_a": w_ffn_in[l][:, :D_FF].astype(BF16), "w_ffn_b": w_ffn_in[l][:, D_FF:].astype(BF16),
            "w_ffn_o": w_ffn_out[l].astype(BF16),
        }
        mk_p, mv_p = _memkv(mem2d, row(mem_norm[l]), w_mem_kv[l].astype(BF16), tm=256)
        mk_p3 = mk_p.reshape(bp, MEM_LEN, D_MODEL)
        mv_p3 = mv_p.reshape(bp, MEM_LEN, D_MODEL)
        yp, (c1, n1, m1), buf1 = _layer(yp, lw, None, None, mk_p3, mv_p3,
                                        batch=bp, seq=sp, tm=256, L=256, tt_xa=512)
        s_state = (state_mlstm_c[l].astype(F32), state_mlstm_n[l].astype(F32), state_mlstm_m[l].astype(F32))
        ys, (c2, n2, m2), buf2 = _layer(ys, lw, s_state, state_conv[l].astype(F32),
                                        cache_mem_k[l].reshape(bs, MEM_LEN, D_MODEL),
                                        cache_mem_v[l].reshape(bs, MEM_LEN, D_MODEL),
                                        batch=bs, seq=ss, tm=bs * ss, L=ss, tt_xa=ss)
        pc.append(c1); pn.append(n1); pm.append(m1); pconv.append(buf1)
        pmk.append(mk_p.reshape(bp, MEM_LEN, XA_HEADS, XA_DH)); pmv.append(mv_p.reshape(bp, MEM_LEN, XA_HEADS, XA_DH))
        sc.append(c2); sn.append(n2); sm.append(m2); sconv.append(buf2)
    return (yp.reshape(bp, sp, D_MODEL), ys.reshape(bs, ss, D_MODEL),
            jnp.stack(pc), jnp.stack(pn), jnp.stack(pm), jnp.stack(pconv), jnp.stack(pmk), jnp.stack(pmv),
            jnp.stack(sc), jnp.stack(sn), jnp.stack(sm), jnp.stack(sconv))
```

```python
import functools

import jax
import jax.numpy as jnp
from jax import lax
from jax.experimental import pallas as pl
from jax.experimental.pallas import tpu as pltpu

F32 = jnp.float32
BF16 = jnp.bfloat16

D_MODEL = 1024
ML_HEADS = 4
ML_DH = 256
XA_HEADS = 4
XA_DH = 256
MEM_LEN = 256
CONV_W = 31
CONV_BUF = CONV_W - 1
D_FF = 2816
RMS_EPS = 1e-6
LN_EPS = 1e-5

N_MAIN = 10 * D_MODEL
COL_Q, COL_K, COL_V, COL_O, COL_GA, COL_GB, COL_XQ, COL_GATE = 0, 1, 2, 3, 4, 5, 6, 7
N_GATES = 2 * ML_HEADS
GATE_PAD = 128
GATE_ROWS = 16

V7X_VMEM_LIMIT_BYTES = 56 * 1024 * 1024
SUBLANES = 8
LANES = 128
CONV_PAD = 32
CONV_POS_GROUP = 18

NT_DIMS = (((1,), (1,)), ((), ()))
TN_DIMS = (((0,), (0,)), ((), ()))


def _tiles(batch, seq):
    if seq >= 512:
        return dict(tm=256, L=256, tt_xa=512)
    return dict(tm=batch * seq, L=seq, tt_xa=seq)


def _resident(block_shape, index_map):
    return pl.BlockSpec(block_shape, index_map, pipeline_mode=pl.Buffered(1))


def _layer_vec(l, width=D_MODEL):
    return pl.BlockSpec((None, 1, width), lambda *_: (l, 0, 0))


def _params(semantics):
    return pltpu.CompilerParams(dimension_semantics=semantics, vmem_limit_bytes=V7X_VMEM_LIMIT_BYTES)


_sigmoid = jax.nn.sigmoid


def _rms(x, g):
    return x * lax.rsqrt(jnp.mean(x * x, axis=-1, keepdims=True) + RMS_EPS) * g


def _inproj_kernel(x_ref, g_ref, w_ref, wg_ref, wgt_ref, p_ref, go_ref, gto_ref, *, chunk):
    xn = _rms(x_ref[...], g_ref[...]).astype(BF16)
    for j in range(N_MAIN // chunk):
        sl = slice(j * chunk, (j + 1) * chunk)
        p_ref[:, sl] = jnp.dot(xn, w_ref[:, sl], preferred_element_type=F32).astype(BF16)
    go_ref[...] = jnp.dot(xn, wg_ref[...], preferred_element_type=F32)
    gto_ref[...] = lax.dot_general(wgt_ref[...], xn, NT_DIMS, preferred_element_type=F32)


def _inproj(x2d, wts, l, *, batch, seq, tm):
    m = x2d.shape[0]
    if seq % tm == 0:
        tiles_per_seq = seq // tm
        gt_shape = (batch, GATE_ROWS, seq)
        gt_spec = pl.BlockSpec((None, GATE_ROWS, tm), lambda i: (i // tiles_per_seq, 0, i % tiles_per_seq))
    else:
        assert m == tm
        gt_shape = (1, GATE_ROWS, m)
        gt_spec = pl.BlockSpec((None, GATE_ROWS, tm), lambda i: (0, 0, 0))
    p, go, gto = pl.pallas_call(
        functools.partial(_inproj_kernel, chunk=D_MODEL),
        grid=(m // tm,),
        in_specs=[
            pl.BlockSpec((tm, D_MODEL), lambda i: (i, 0)),
            _layer_vec(l),
            _resident((None, D_MODEL, N_MAIN), lambda i: (l, 0, 0)),
            _resident((None, D_MODEL, GATE_PAD), lambda i: (l, 0, 0)),
            _resident((None, GATE_ROWS, D_MODEL), lambda i: (l, 0, 0)),
        ],
        out_specs=[
            pl.BlockSpec((tm, N_MAIN), lambda i: (i, 0)),
            pl.BlockSpec((tm, GATE_PAD), lambda i: (i, 0)),
            gt_spec,
        ],
        out_shape=[
            jax.ShapeDtypeStruct((m, N_MAIN), BF16),
            jax.ShapeDtypeStruct((m, GATE_PAD), F32),
            jax.ShapeDtypeStruct(gt_shape, F32),
        ],
        compiler_params=_params(("parallel",)),
        name="inproj",
    )(x2d, wts["g_mix_pre"], wts["w_main"], wts["w_gate"], wts["w_gate_t"])
    if seq % tm != 0:
        gto = jnp.transpose(gto.reshape(GATE_ROWS, batch, seq), (1, 0, 2))
    return p, go, gto


def _mlstm_kernel(*refs, L, has_state):
    if has_state:
        (q_ref, k_ref, v_ref, o_ref, g_ref, gt_ref, brow_ref, bcol_ref, gh_ref,
         c0_ref, n0_ref, m0_ref, y_ref, c_ref, n_ref, m_ref) = refs
    else:
        (q_ref, k_ref, v_ref, o_ref, g_ref, gt_ref, brow_ref, bcol_ref, gh_ref,
         y_ref, c_ref, n_ref, m_ref) = refs
    t = pl.program_id(1)

    @pl.when(t == 0)
    def _():
        if has_state:
            c_ref[...] = c0_ref[...]
            n_ref[...] = n0_ref[...]
            m_ref[...] = m0_ref[...]
        else:
            c_ref[...] = jnp.zeros_like(c_ref)
            n_ref[...] = jnp.zeros_like(n_ref)
            m_ref[...] = jnp.zeros_like(m_ref)

    row = lax.broadcasted_iota(jnp.int32, (L, L), 0)
    col = lax.broadcasted_iota(jnp.int32, (L, L), 1)
    causal = col <= row
    tril = causal.astype(F32)
    triu = (row <= col).astype(F32)

    gates_c = g_ref[...] + brow_ref[...]
    gates_r = gt_ref[...] + bcol_ref[...]
    cum_c = jnp.dot(tril, jax.nn.log_sigmoid(gates_c), precision=lax.Precision.HIGHEST,
                    preferred_element_type=F32)
    cum_r = jnp.dot(jax.nn.log_sigmoid(gates_r), triu, precision=lax.Precision.HIGHEST,
                    preferred_element_type=F32)

    for h in range(ML_HEADS):
        hs = slice(h * ML_DH, (h + 1) * ML_DH)
        ig_c = gates_c[:, h:h + 1]
        b_c = cum_c[:, ML_HEADS + h:ML_HEADS + h + 1]
        src_r = cum_r[ML_HEADS + h:ML_HEADS + h + 1, :] - gates_r[h:h + 1, :]
        m_prev = m_ref[:, h:h + 1]

        d = jnp.where(causal, b_c - src_r, -jnp.inf)
        a = b_c + m_prev
        m_row = jnp.maximum(a, jnp.max(d, axis=1, keepdims=True))

        q = q_ref[:, hs]
        ks = k_ref[:, hs] * jnp.asarray(ML_DH ** -0.5, BF16)
        v = v_ref[:, hs]
        s = lax.dot_general(q, ks, NT_DIMS, preferred_element_type=F32) * jnp.exp(d - m_row)
        w_inter = jnp.exp(a - m_row)

        c = c_ref[h]
        n = n_ref[h:h + 1, :]
        qc = lax.dot_general(q, c.astype(BF16), NT_DIMS, preferred_element_type=F32)
        num = w_inter * qc + jnp.dot(s.astype(BF16), v, preferred_element_type=F32)
        qn = jnp.sum(q.astype(F32) * n, axis=1, keepdims=True)
        den = w_inter * qn + jnp.sum(s, axis=1, keepdims=True)
        hh = num * (1.0 / jnp.maximum(jnp.abs(den), jnp.exp(-m_row)))

        b_last = b_c[L - 1:L, :]
        g_c = b_last - b_c + ig_c
        m_new = jnp.maximum(b_last + m_prev, jnp.max(g_c, axis=0, keepdims=True))
        w_old = jnp.exp(b_last + m_prev - m_new)
        w_s = jnp.exp(g_c - m_new)
        wv = (w_s * v.astype(F32)).astype(BF16)
        c_ref[h] = w_old * c + lax.dot_general(wv, ks, TN_DIMS, preferred_element_type=F32)
        n_ref[h:h + 1, :] = w_old * n + jnp.sum(w_s * ks.astype(F32), axis=0, keepdims=True)
        m_ref[:, h:h + 1] = m_new

        hn = hh * lax.rsqrt(jnp.mean(hh * hh, axis=1, keepdims=True) + RMS_EPS) * gh_ref[:, hs]
        y_ref[:, hs] = (hn * _sigmoid(o_ref[:, hs].astype(F32))).astype(BF16)


def _mlstm(p, go, gto, wts, l, state, *, batch, seq, L):
    nb = seq // L
    m = batch * seq
    has_state = state is not None
    rows = lambda b, t: b * nb + t
    in_specs = [
        pl.BlockSpec((L, D_MODEL), lambda b, t: (rows(b, t), COL_Q)),
        pl.BlockSpec((L, D_MODEL), lambda b, t: (rows(b, t), COL_K)),
        pl.BlockSpec((L, D_MODEL), lambda b, t: (rows(b, t), COL_V)),
        pl.BlockSpec((L, D_MODEL), lambda b, t: (rows(b, t), COL_O)),
        pl.BlockSpec((L, GATE_PAD), lambda b, t: (rows(b, t), 0)),
        pl.BlockSpec((None, GATE_ROWS, L), lambda b, t: (b, 0, t)),
        _layer_vec(l, GATE_PAD),
        pl.BlockSpec((None, GATE_ROWS, 1), lambda b, t: (l, 0, 0)),
        _layer_vec(l),
    ]
    args = [p, p, p, p, go, gto, wts["bias_row"], wts["bias_col"], wts["g_mhead"]]
    out_state_specs = [
        pl.BlockSpec((None, ML_HEADS, ML_DH, ML_DH), lambda b, t: (b, 0, 0, 0)),
        pl.BlockSpec((None, ML_HEADS, ML_DH), lambda b, t: (b, 0, 0)),
        pl.BlockSpec((None, 1, ML_HEADS), lambda b, t: (b, 0, 0)),
    ]
    if has_state:
        c0, n0, m0 = state
        in_specs += [
            pl.BlockSpec((None, None, ML_HEADS, ML_DH, ML_DH), lambda b, t: (l, b, 0, 0, 0)),
            pl.BlockSpec((None, None, ML_HEADS, ML_DH), lambda b, t: (l, b, 0, 0)),
            pl.BlockSpec((None, None, 1, ML_HEADS), lambda b, t: (l, b, 0, 0)),
        ]
        args += [c0, n0, m0]
    y, c, n, mm = pl.pallas_call(
        functools.partial(_mlstm_kernel, L=L, has_state=has_state),
        grid=(batch, nb),
        in_specs=in_specs,
        out_specs=[pl.BlockSpec((L, D_MODEL), lambda b, t: (rows(b, t), 0))] + out_state_specs,
        out_shape=[
            jax.ShapeDtypeStruct((m, D_MODEL), BF16),
            jax.ShapeDtypeStruct((batch, ML_HEADS, ML_DH, ML_DH), F32),
            jax.ShapeDtypeStruct((batch, ML_HEADS, ML_DH), F32),
            jax.ShapeDtypeStruct((batch, 1, ML_HEADS), F32),
        ],
        compiler_params=_params(("parallel", "arbitrary")),
        name="mlstm",
    )(*args)
    return y, c, n, mm.reshape(batch, ML_HEADS)


def _conv_kernel(*refs, tt, has_state):
    if has_state:
        ga_ref, gb_ref, w_ref, cb_ref, lng_ref, lnb_ref, s0_ref, y_ref, so_ref, ext_ref, out_ref = refs
    else:
        ga_ref, gb_ref, w_ref, cb_ref, lng_ref, lnb_ref, y_ref, so_ref, ext_ref, out_ref = refs
    t = pl.program_id(1)
    lo = CONV_PAD - CONV_BUF
    seg = (tt + CONV_PAD) // SUBLANES
    n_slabs = D_MODEL // LANES

    @pl.when(t == 0)
    def _():
        ext_ref[...] = jnp.zeros_like(ext_ref)
        if has_state:
            for c in range(n_slabs):
                ext_ref[c, lo:CONV_PAD, :] = s0_ref[:, c * LANES:(c + 1) * LANES]

    u = ga_ref[...].astype(F32) * _sigmoid(gb_ref[...].astype(F32))
    for c in range(n_slabs):
        ext_ref[c, CONV_PAD:CONV_PAD + tt, :] = u[:, c * LANES:(c + 1) * LANES]

    groups = [(p0, min(p0 + CONV_POS_GROUP, seg)) for p0 in range(0, seg, CONV_POS_GROUP)]
    for c in range(n_slabs):
        cs = slice(c * LANES, (c + 1) * LANES)
        wv = [jnp.broadcast_to(w_ref[j:j + 1, cs], (SUBLANES, LANES)) for j in range(CONV_W)]
        bias = jnp.broadcast_to(cb_ref[:, cs], (SUBLANES, LANES))
        for p0, p1 in groups:
            acc = [None] * (p1 - p0)
            for q in range(p0, p1 + CONV_W - 1):
                z = ext_ref[c, pl.ds(q, SUBLANES, stride=seg), :]
                for p in range(max(p0, q - CONV_W + 1), min(p1, q + 1)):
                    term = z * wv[q - p]
                    acc[p - p0] = term if acc[p - p0] is None else acc[p - p0] + term
            for p in range(p0, p1):
                out_ref[c, pl.ds(p + CONV_BUF, SUBLANES, stride=seg), :] = acc[p - p0] + bias

    for c in range(n_slabs):
        so_ref[:, c * LANES:(c + 1) * LANES] = ext_ref[c, tt + lo:tt + CONV_PAD, :]
        ext_ref[c, 0:CONV_PAD, :] = ext_ref[c, tt:tt + CONV_PAD, :]

    x = jnp.concatenate([out_ref[c, CONV_PAD:CONV_PAD + tt, :] for c in range(n_slabs)], axis=1)
    mu = jnp.mean(x, axis=-1, keepdims=True)
    xc = x - mu
    yn = xc * lax.rsqrt(jnp.mean(xc * xc, axis=-1, keepdims=True) + LN_EPS) * lng_ref[...] + lnb_ref[...]
    y_ref[...] = (yn * _sigmoid(yn)).astype(BF16)


def _conv(p, wts, l, state, *, batch, seq, tt):
    nt = seq // tt
    m = batch * seq
    has_state = state is not None
    rows = lambda b, t: b * nt + t
    in_specs = [
        pl.BlockSpec((tt, D_MODEL), lambda b, t: (rows(b, t), COL_GA)),
        pl.BlockSpec((tt, D_MODEL), lambda b, t: (rows(b, t), COL_GB)),
        pl.BlockSpec((None, CONV_W, D_MODEL), lambda b, t: (l, 0, 0)),
        _layer_vec(l), _layer_vec(l), _layer_vec(l),
    ]
    args = [p, p, wts["conv_w"], wts["conv_b"], wts["ln_g"], wts["ln_b"]]
    if has_state:
        in_specs.append(pl.BlockSpec((None, None, CONV_BUF, D_MODEL), lambda b, t: (l, b, 0, 0)))
        args.append(state)
    y, so = pl.pallas_call(
        functools.partial(_conv_kernel, tt=tt, has_state=has_state),
        grid=(batch, nt),
        in_specs=in_specs,
        out_specs=[pl.BlockSpec((tt, D_MODEL), lambda b, t: (rows(b, t), 0)),
                   pl.BlockSpec((None, CONV_BUF, D_MODEL), lambda b, t: (b, 0, 0))],
        out_shape=[
            jax.ShapeDtypeStruct((m, D_MODEL), BF16),
            jax.ShapeDtypeStruct((batch, CONV_BUF, D_MODEL), F32),
        ],
        scratch_shapes=[
            pltpu.VMEM((D_MODEL // LANES, tt + 2 * CONV_PAD, LANES), F32),
            pltpu.VMEM((D_MODEL // LANES, tt + 2 * CONV_PAD, LANES), F32),
        ],
        compiler_params=_params(("parallel", "arbitrary")),
        name="conv",
    )(*args)
    return y, so


def _xattn_kernel(q_ref, k_ref, v_ref, y_ref, kb_ref, vb_ref):
    @pl.when(pl.program_id(1) == 0)
    def _():
        kb_ref[...] = k_ref[...].astype(BF16)
        vb_ref[...] = v_ref[...].astype(BF16)

    for h in range(XA_HEADS):
        hs = slice(h * XA_DH, (h + 1) * XA_DH)
        s = lax.dot_general(q_ref[:, hs], kb_ref[:, hs], NT_DIMS, preferred_element_type=F32)
        s = s * (XA_DH ** -0.5)
        e = jnp.exp(s - jnp.max(s, axis=-1, keepdims=True))
        p = e * (1.0 / jnp.sum(e, axis=-1, keepdims=True))
        y_ref[:, hs] = jnp.dot(p.astype(BF16), vb_ref[:, hs], preferred_element_type=F32).astype(BF16)


def _xattn(p, mk, mv, kv_index, *, batch, seq, tt):
    nt = seq // tt
    m = batch * seq
    kv_block = (None,) * (mk.ndim - 2) + (MEM_LEN, D_MODEL)
    kv_spec = pl.BlockSpec(kv_block, lambda b, t: kv_index(b))
    return pl.pallas_call(
        _xattn_kernel,
        grid=(batch, nt),
        in_specs=[pl.BlockSpec((tt, D_MODEL), lambda b, t: (b * nt + t, COL_XQ)), kv_spec, kv_spec],
        out_specs=pl.BlockSpec((tt, D_MODEL), lambda b, t: (b * nt + t, 0)),
        out_shape=jax.ShapeDtypeStruct((m, D_MODEL), BF16),
        scratch_shapes=[pltpu.VMEM((MEM_LEN, D_MODEL), BF16), pltpu.VMEM((MEM_LEN, D_MODEL), BF16)],
        compiler_params=_params(("parallel", "arbitrary")),
        name="xattn",
    )(p, mk, mv)


def _merge_kernel(x_ref, yml_ref, ycv_ref, yxa_ref, g0_ref, g1_ref, g2_ref, wb_ref, wo_ref, gp_ref, o_ref):
    mixed = None
    for i, (y_ref, gate_ref) in enumerate(((yml_ref, g0_ref), (ycv_ref, g1_ref), (yxa_ref, g2_ref))):
        term = _sigmoid(gate_ref[...].astype(F32)) * jnp.dot(y_ref[...], wb_ref[i], preferred_element_type=F32)
        mixed = term if mixed is None else mixed + term
    z = jnp.dot(mixed.astype(BF16), wo_ref[...], preferred_element_type=F32)
    o_ref[...] = x_ref[...] + _rms(z, gp_ref[...])


def _merge(x2d, y_ml, y_cv, y_xa, p, wts, l, *, tm):
    m = x2d.shape[0]
    tile = lambda c: pl.BlockSpec((tm, D_MODEL), lambda i: (i, c))
    return pl.pallas_call(
        _merge_kernel,
        grid=(m // tm,),
        in_specs=[
            tile(0), tile(0), tile(0), tile(0),
            tile(COL_GATE), tile(COL_GATE + 1), tile(COL_GATE + 2),
            _resident((None, 3, D_MODEL, D_MODEL), lambda i: (l, 0, 0, 0)),
            _resident((None, D_MODEL, D_MODEL), lambda i: (l, 0, 0)),
            _layer_vec(l),
        ],
        out_specs=tile(0),
        out_shape=jax.ShapeDtypeStruct((m, D_MODEL), F32),
        compiler_params=_params(("parallel",)),
        name="merge",
    )(x2d, y_ml, y_cv, y_xa, p, p, p, wts["w_branch"], wts["w_out"], wts["g_mix_post"])


FFN_CHUNKS = ((0, 768), (768, 768), (1536, 768), (2304, 512))


def _ffn_kernel(x_ref, gpre_ref, wa_ref, wb_ref, wo_ref, gpost_ref, o_ref, h_ref):
    x = x_ref[...]
    xn = _rms(x, gpre_ref[...]).astype(BF16)
    for start, size in FFN_CHUNKS:
        sl = slice(start, start + size)
        fa = jnp.dot(xn, wa_ref[:, sl], preferred_element_type=F32)
        fb = jnp.dot(xn, wb_ref[:, sl], preferred_element_type=F32)
        h_ref[:, sl] = (fa * _sigmoid(fa) * fb).astype(BF16)
    z = jnp.dot(h_ref[...], wo_ref[...], preferred_element_type=F32)
    o_ref[...] = x + _rms(z, gpost_ref[...])


def _ffn(x2d, wts, l, *, tm):
    m = x2d.shape[0]
    return pl.pallas_call(
        _ffn_kernel,
        grid=(m // tm,),
        in_specs=[
            pl.BlockSpec((tm, D_MODEL), lambda i: (i, 0)),
            _layer_vec(l),
            _resident((None, D_MODEL, D_FF), lambda i: (l, 0, 0)),
            _resident((None, D_MODEL, D_FF), lambda i: (l, 0, 1)),
            _resident((None, D_FF, D_MODEL), lambda i: (l, 0, 0)),
            _layer_vec(l),
        ],
        out_specs=pl.BlockSpec((tm, D_MODEL), lambda i: (i, 0)),
        out_shape=jax.ShapeDtypeStruct((m, D_MODEL), F32),
        scratch_shapes=[pltpu.VMEM((tm, D_FF), BF16)],
        compiler_params=_params(("parallel",)),
        name="ffn",
    )(x2d, wts["g_ffn_pre"], wts["w_ffn_in"], wts["w_ffn_in"], wts["w_ffn_out"], wts["g_ffn_post"])


def _memkv_kernel(x_ref, g_ref, w_ref, k_ref, v_ref):
    xn = _rms(x_ref[...], g_ref[...]).astype(BF16)
    k_ref[...] = jnp.dot(xn, w_ref[:, :D_MODEL], preferred_element_type=F32)
    v_ref[...] = jnp.dot(xn, w_ref[:, D_MODEL:], preferred_element_type=F32)


def _memkv(mem2d, wts, l, *, tm):
    m = mem2d.shape[0]
    out = jax.ShapeDtypeStruct((m, D_MODEL), F32)
    tile = pl.BlockSpec((tm, D_MODEL), lambda i: (i, 0))
    return pl.pallas_call(
        _memkv_kernel,
        grid=(m // tm,),
        in_specs=[tile, _layer_vec(l), _resident((None, D_MODEL, 2 * D_MODEL), lambda i: (l, 0, 0))],
        out_specs=[tile, tile],
        out_shape=[out, out],
        compiler_params=_params(("parallel",)),
        name="memkv",
    )(mem2d, wts["g_mem"], wts["w_mem_kv"])


def _layer(x2d, wts, l, ml_state, conv_state, mk, mv, kv_index, *, batch, seq):
    tiles = _tiles(batch, seq)
    tm, L = tiles["tm"], tiles["L"]
    p, go, gto = _inproj(x2d, wts, l, batch=batch, seq=seq, tm=tm)
    y_ml, c, n, mm = _mlstm(p, go, gto, wts, l, ml_state, batch=batch, seq=seq, L=L)
    y_cv, conv_out = _conv(p, wts, l, conv_state, batch=batch, seq=seq, tt=L)
    y_xa = _xattn(p, mk, mv, kv_index, batch=batch, seq=seq, tt=tiles["tt_xa"])
    x2d = _merge(x2d, y_ml, y_cv, y_xa, p, wts, l, tm=tm)
    x2d = _ffn(x2d, wts, l, tm=tm)
    return x2d, (c, n, mm), conv_out


def kernel(x_prompt, x_sample, mem_prompt, state_mlstm_c, state_mlstm_n, state_mlstm_m, state_conv,
           cache_mem_k, cache_mem_v, norm_mix_pre, norm_mix_post, norm_ffn_pre, norm_ffn_post, w_in, b_gate,
           mlstm_head_norm, conv_w, conv_b, conv_ln_g, conv_ln_b, mem_norm, w_mem_kv, w_branch, w_out,
           w_ffn_in, w_ffn_out):
    depth = w_in.shape[0]
    bp, sp, _ = x_prompt.shape
    bs, ss, _ = x_sample.shape
    gate_lo = 4 * D_MODEL
    gate_hi = gate_lo + N_GATES

    vec = lambda v: v.reshape(depth, 1, -1).astype(F32)
    w_gate_cols = w_in[:, :, gate_lo:gate_hi]
    wts = {
        "w_main": jnp.concatenate([w_in[:, :, :gate_lo].astype(BF16), w_in[:, :, gate_hi:].astype(BF16)], axis=2),
        "w_gate": jnp.pad(w_gate_cols, ((0, 0), (0, 0), (0, GATE_PAD - N_GATES))).astype(BF16),
        "w_gate_t": jnp.pad(jnp.swapaxes(w_gate_cols, 1, 2), ((0, 0), (0, GATE_ROWS - N_GATES), (0, 0))).astype(BF16),
        "bias_row": jnp.pad(b_gate, ((0, 0), (0, GATE_PAD - N_GATES))).reshape(depth, 1, GATE_PAD).astype(F32),
        "bias_col": jnp.pad(b_gate, ((0, 0), (0, GATE_ROWS - N_GATES))).reshape(depth, GATE_ROWS, 1).astype(F32),
        "g_mix_pre": vec(norm_mix_pre), "g_mix_post": vec(norm_mix_post),
        "g_ffn_pre": vec(norm_ffn_pre), "g_ffn_post": vec(norm_ffn_post),
        "g_mhead": vec(mlstm_head_norm), "g_mem": vec(mem_norm),
        "conv_w": conv_w.astype(F32), "conv_b": vec(conv_b), "ln_g": vec(conv_ln_g), "ln_b": vec(conv_ln_b),
        "w_branch": w_branch.astype(BF16), "w_out": w_out.astype(BF16),
        "w_ffn_in": w_ffn_in.astype(BF16), "w_ffn_out": w_ffn_out.astype(BF16),
        "w_mem_kv": w_mem_kv.astype(BF16),
    }
    s_state = (state_mlstm_c.astype(F32), state_mlstm_n.astype(F32),
               state_mlstm_m.astype(F32).reshape(depth, bs, 1, ML_HEADS))
    s_conv = state_conv.astype(F32)
    cache_k = cache_mem_k.reshape(depth, bs, MEM_LEN, D_MODEL)
    cache_v = cache_mem_v.reshape(depth, bs, MEM_LEN, D_MODEL)

    yp = x_prompt.reshape(bp * sp, D_MODEL)
    ys = x_sample.reshape(bs * ss, D_MODEL)
    mem2d = mem_prompt.reshape(bp * MEM_LEN, D_MODEL)

    pc, pn, pm, pconv, pmk, pmv = [], [], [], [], [], []
    sc, sn, sm, sconv = [], [], [], []
    for l in range(depth):
        mk_p, mv_p = _memkv(mem2d, wts, l, tm=256)
        yp, (c1, n1, m1), buf1 = _layer(yp, wts, l, None, None,
                                        mk_p.reshape(bp, MEM_LEN, D_MODEL), mv_p.reshape(bp, MEM_LEN, D_MODEL),
                                        lambda b: (b, 0, 0), batch=bp, seq=sp)
        ys, (c2, n2, m2), buf2 = _layer(ys, wts, l, s_state, s_conv, cache_k, cache_v,
                                        functools.partial(lambda b, l_: (l_, b, 0, 0), l_=l), batch=bs, seq=ss)
        pc.append(c1); pn.append(n1); pm.append(m1); pconv.append(buf1)
        pmk.append(mk_p.reshape(bp, MEM_LEN, XA_HEADS, XA_DH)); pmv.append(mv_p.reshape(bp, MEM_LEN, XA_HEADS, XA_DH))
        sc.append(c2); sn.append(n2); sm.append(m2); sconv.append(buf2)
    return (yp.reshape(bp, sp, D_MODEL), ys.reshape(bs, ss, D_MODEL),
            jnp.stack(pc), jnp.stack(pn), jnp.stack(pm), jnp.stack(pconv), jnp.stack(pmk), jnp.stack(pmv),
            jnp.stack(sc), jnp.stack(sn), jnp.stack(sm), jnp.stack(sconv))
```

```python
import functools

import jax
import jax.numpy as jnp
from jax import lax
from jax.experimental import pallas as pl
from jax.experimental.pallas import tpu as pltpu

F32 = jnp.float32
BF16 = jnp.bfloat16

D_MODEL = 1024
ML_HEADS = 4
ML_DH = 256
XA_HEADS = 4
XA_DH = 256
MEM_LEN = 256
CONV_W = 31
CONV_BUF = CONV_W - 1
D_FF = 2816
RMS_EPS = 1e-6
LN_EPS = 1e-5

N_MAIN = 10 * D_MODEL
COL_Q, COL_K, COL_V, COL_O, COL_GA, COL_GB, COL_XQ, COL_GATE = 0, 1, 2, 3, 4, 5, 6, 7
N_BRANCH = 3
N_GATES = 2 * ML_HEADS
GATE_PAD = 128
GATE_ROWS = 16
FUSED_COLS = (COL_Q, COL_K, COL_V, COL_O, COL_XQ)

V7X_VMEM_LIMIT_BYTES = 56 * 1024 * 1024
SUBLANES = 8
LANES = 128
CONV_PAD = 32
CONV_POS_GROUP = 12

NT_DIMS = (((1,), (1,)), ((), ()))
TN_DIMS = (((0,), (0,)), ((), ()))

_sigmoid = jax.nn.sigmoid


def _tiles(batch, seq):
    if seq >= 512:
        return dict(tm=256, L=256, tt_xa=512)
    return dict(tm=batch * seq, L=seq, tt_xa=seq)


def _resident(block_shape, index_map):
    return pl.BlockSpec(block_shape, index_map, pipeline_mode=pl.Buffered(1))


def _layer_vec(l, width=D_MODEL):
    return pl.BlockSpec((None, 1, width), lambda *_: (l, 0, 0))


def _params(semantics):
    return pltpu.CompilerParams(dimension_semantics=semantics, vmem_limit_bytes=V7X_VMEM_LIMIT_BYTES)


def _col(c):
    return slice(c * D_MODEL, (c + 1) * D_MODEL)


def _rms(x, g):
    return x * lax.rsqrt(jnp.mean(x * x, axis=-1, keepdims=True) + RMS_EPS) * g


def _mlstm_body(q_src, k_src, v_src, o_src, gates_c, gates_r, gh_ref, y_ref, c_ref, n_ref, m_ref, *, L):
    row = lax.broadcasted_iota(jnp.int32, (L, L), 0)
    col = lax.broadcasted_iota(jnp.int32, (L, L), 1)
    causal = col <= row
    tril = causal.astype(F32)
    triu = (row <= col).astype(F32)
    cum_c = jnp.dot(tril, jax.nn.log_sigmoid(gates_c), precision=lax.Precision.HIGHEST,
                    preferred_element_type=F32)
    cum_r = jnp.dot(jax.nn.log_sigmoid(gates_r), triu, precision=lax.Precision.HIGHEST,
                    preferred_element_type=F32)

    def head_cols(src, h):
        ref, first = src
        return ref[:, first + h * ML_DH:first + (h + 1) * ML_DH]

    for h in range(ML_HEADS):
        hs = slice(h * ML_DH, (h + 1) * ML_DH)
        ig_c = gates_c[:, h:h + 1]
        b_c = cum_c[:, ML_HEADS + h:ML_HEADS + h + 1]
        src_r = cum_r[ML_HEADS + h:ML_HEADS + h + 1, :] - gates_r[h:h + 1, :]
        m_prev = m_ref[:, h:h + 1]

        d = jnp.where(causal, b_c - src_r, -jnp.inf)
        a = b_c + m_prev
        m_row = jnp.maximum(a, jnp.max(d, axis=1, keepdims=True))

        q = head_cols(q_src, h)
        ks = head_cols(k_src, h) * jnp.asarray(ML_DH ** -0.5, BF16)
        v = head_cols(v_src, h)
        s = lax.dot_general(q, ks, NT_DIMS, preferred_element_type=F32) * jnp.exp(d - m_row)
        w_inter = jnp.exp(a - m_row)

        c = c_ref[h]
        n = n_ref[h:h + 1, :]
        qc = lax.dot_general(q, c.astype(BF16), NT_DIMS, preferred_element_type=F32)
        num = w_inter * qc + jnp.dot(s.astype(BF16), v, preferred_element_type=F32)
        qn = jnp.sum(q.astype(F32) * n, axis=1, keepdims=True)
        den = w_inter * qn + jnp.sum(s, axis=1, keepdims=True)
        hh = num * (1.0 / jnp.maximum(jnp.abs(den), jnp.exp(-m_row)))

        b_last = b_c[L - 1:L, :]
        g_c = b_last - b_c + ig_c
        m_new = jnp.maximum(b_last + m_prev, jnp.max(g_c, axis=0, keepdims=True))
        w_old = jnp.exp(b_last + m_prev - m_new)
        w_s = jnp.exp(g_c - m_new)
        wv = (w_s * v.astype(F32)).astype(BF16)
        c_ref[h] = w_old * c + lax.dot_general(wv, ks, TN_DIMS, preferred_element_type=F32)
        n_ref[h:h + 1, :] = w_old * n + jnp.sum(w_s * ks.astype(F32), axis=0, keepdims=True)
        m_ref[:, h:h + 1] = m_new

        hn = hh * lax.rsqrt(jnp.mean(hh * hh, axis=1, keepdims=True) + RMS_EPS) * gh_ref[:, hs]
        y_ref[:, hs] = (hn * _sigmoid(head_cols(o_src, h).astype(F32))).astype(BF16)


def _conv_init(ext_ref, s0_ref):
    ext_ref[...] = jnp.zeros_like(ext_ref)
    if s0_ref is not None:
        for c in range(D_MODEL // LANES):
            ext_ref[c, CONV_PAD - CONV_BUF:CONV_PAD, :] = s0_ref[:, c * LANES:(c + 1) * LANES]


def _conv_body(u, w_ref, cb_ref, lng_ref, lnb_ref, y_ref, so_ref, ext_ref, out_ref, *, tt):
    lo = CONV_PAD - CONV_BUF
    seg = (tt + CONV_PAD) // SUBLANES
    n_slabs = D_MODEL // LANES
    for c in range(n_slabs):
        ext_ref[c, CONV_PAD:CONV_PAD + tt, :] = u[:, c * LANES:(c + 1) * LANES]

    groups = [(p0, min(p0 + CONV_POS_GROUP, seg)) for p0 in range(0, seg, CONV_POS_GROUP)]
    for c in range(n_slabs):
        cs = slice(c * LANES, (c + 1) * LANES)
        bias = jnp.broadcast_to(cb_ref[:, cs], (SUBLANES, LANES))
        for p0, p1 in groups:
            acc = [None] * (p1 - p0)
            for q in range(p0, p1 + CONV_W - 1):
                z = ext_ref[c, pl.ds(q, SUBLANES, stride=seg), :]
                for p in range(max(p0, q - CONV_W + 1), min(p1, q + 1)):
                    term = z * w_ref[q - p, :, cs]
                    acc[p - p0] = term if acc[p - p0] is None else acc[p - p0] + term
            for p in range(p0, p1):
                out_ref[c, pl.ds(p + CONV_BUF, SUBLANES, stride=seg), :] = acc[p - p0] + bias

    for c in range(n_slabs):
        so_ref[:, c * LANES:(c + 1) * LANES] = ext_ref[c, tt + lo:tt + CONV_PAD, :]
        ext_ref[c, 0:CONV_PAD, :] = ext_ref[c, tt:tt + CONV_PAD, :]

    x = jnp.concatenate([out_ref[c, CONV_PAD:CONV_PAD + tt, :] for c in range(n_slabs)], axis=1)
    mu = jnp.mean(x, axis=-1, keepdims=True)
    xc = x - mu
    yn = xc * lax.rsqrt(jnp.mean(xc * xc, axis=-1, keepdims=True) + LN_EPS) * lng_ref[...] + lnb_ref[...]
    y_ref[...] = (yn * _sigmoid(yn)).astype(BF16)


def _conv_scratch(tt):
    slab = pltpu.VMEM((D_MODEL // LANES, tt + 2 * CONV_PAD, LANES), F32)
    return [slab, slab]


def _xattn_body(q_src, kb_ref, vb_ref, y_ref):
    q_ref, first = q_src
    for h in range(XA_HEADS):
        hs = slice(h * XA_DH, (h + 1) * XA_DH)
        q = q_ref[:, first + h * XA_DH:first + (h + 1) * XA_DH]
        s = lax.dot_general(q, kb_ref[:, hs], NT_DIMS, preferred_element_type=F32)
        s = s * (XA_DH ** -0.5)
        e = jnp.exp(s - jnp.max(s, axis=-1, keepdims=True))
        p = e * (1.0 / jnp.sum(e, axis=-1, keepdims=True))
        y_ref[:, hs] = jnp.dot(p.astype(BF16), vb_ref[:, hs], preferred_element_type=F32).astype(BF16)


def _branches_kernel(x_ref, gpre_ref, w_ref, wg_ref, wgt_ref, brow_ref, bcol_ref, gh_ref,
                     cw_ref, cb_ref, lng_ref, lnb_ref, mk_ref, mv_ref,
                     yml_ref, ycv_ref, yxa_ref, gate_ref, c_ref, n_ref, m_ref, so_ref,
                     p_scr, ext_ref, out_ref, kb_ref, vb_ref, *, tt):
    @pl.when(pl.program_id(1) == 0)
    def _():
        c_ref[...] = jnp.zeros_like(c_ref)
        n_ref[...] = jnp.zeros_like(n_ref)
        m_ref[...] = jnp.zeros_like(m_ref)
        _conv_init(ext_ref, None)
        kb_ref[...] = mk_ref[...].astype(BF16)
        vb_ref[...] = mv_ref[...].astype(BF16)

    xn = _rms(x_ref[...], gpre_ref[...]).astype(BF16)
    proj = lambda c: jnp.dot(xn, w_ref[:, _col(c)], preferred_element_type=F32)
    u = proj(COL_GA) * _sigmoid(proj(COL_GB))
    for j, c in enumerate(FUSED_COLS):
        p_scr[:, _col(j)] = proj(c).astype(BF16)
    for j in range(N_BRANCH):
        gate_ref[:, _col(j)] = proj(COL_GATE + j).astype(BF16)
    gates_c = jnp.dot(xn, wg_ref[...], preferred_element_type=F32) + brow_ref[...]
    gates_r = lax.dot_general(wgt_ref[...], xn, NT_DIMS, preferred_element_type=F32) + bcol_ref[...]

    _conv_body(u, cw_ref, cb_ref, lng_ref, lnb_ref, ycv_ref, so_ref, ext_ref, out_ref, tt=tt)
    src = lambda c: (p_scr, FUSED_COLS.index(c) * D_MODEL)
    _mlstm_body(src(COL_Q), src(COL_K), src(COL_V), src(COL_O), gates_c, gates_r, gh_ref,
                yml_ref, c_ref, n_ref, m_ref, L=tt)
    _xattn_body(src(COL_XQ), kb_ref, vb_ref, yxa_ref)


def _branches(x2d, wts, l, mk, mv, *, batch, seq, tt):
    nt = seq // tt
    m = batch * seq
    rows = lambda b, t: (b * nt + t, 0)
    tile = lambda width: pl.BlockSpec((tt, width), rows)
    kv_spec = pl.BlockSpec((None, MEM_LEN, D_MODEL), lambda b, t: (b, 0, 0))
    y = jax.ShapeDtypeStruct((m, D_MODEL), BF16)
    outs = pl.pallas_call(
        functools.partial(_branches_kernel, tt=tt),
        grid=(batch, nt),
        in_specs=[
            tile(D_MODEL), _layer_vec(l),
            _resident((None, D_MODEL, N_MAIN), lambda b, t: (l, 0, 0)),
            _resident((None, D_MODEL, GATE_PAD), lambda b, t: (l, 0, 0)),
            _resident((None, GATE_ROWS, D_MODEL), lambda b, t: (l, 0, 0)),
            _layer_vec(l, GATE_PAD),
            pl.BlockSpec((None, GATE_ROWS, 1), lambda b, t: (l, 0, 0)),
            _layer_vec(l),
            pl.BlockSpec((None, CONV_W, SUBLANES, D_MODEL), lambda b, t: (l, 0, 0, 0)),
            _layer_vec(l), _layer_vec(l), _layer_vec(l),
            kv_spec, kv_spec,
        ],
        out_specs=[
            tile(D_MODEL), tile(D_MODEL), tile(D_MODEL), tile(N_BRANCH * D_MODEL),
            pl.BlockSpec((None, ML_HEADS, ML_DH, ML_DH), lambda b, t: (b, 0, 0, 0)),
            pl.BlockSpec((None, ML_HEADS, ML_DH), lambda b, t: (b, 0, 0)),
            pl.BlockSpec((None, 1, ML_HEADS), lambda b, t: (b, 0, 0)),
            pl.BlockSpec((None, CONV_BUF, D_MODEL), lambda b, t: (b, 0, 0)),
        ],
        out_shape=[
            y, y, y, jax.ShapeDtypeStruct((m, N_BRANCH * D_MODEL), BF16),
            jax.ShapeDtypeStruct((batch, ML_HEADS, ML_DH, ML_DH), F32),
            jax.ShapeDtypeStruct((batch, ML_HEADS, ML_DH), F32),
            jax.ShapeDtypeStruct((batch, 1, ML_HEADS), F32),
            jax.ShapeDtypeStruct((batch, CONV_BUF, D_MODEL), F32),
        ],
        scratch_shapes=[pltpu.VMEM((tt, len(FUSED_COLS) * D_MODEL), BF16)] + _conv_scratch(tt)
                       + [pltpu.VMEM((MEM_LEN, D_MODEL), BF16), pltpu.VMEM((MEM_LEN, D_MODEL), BF16)],
        compiler_params=_params(("parallel", "arbitrary")),
        name="branches",
    )(x2d, wts["g_mix_pre"], wts["w_main"], wts["w_gate"], wts["w_gate_t"], wts["bias_row"], wts["bias_col"],
      wts["g_mhead"], wts["conv_w"], wts["conv_b"], wts["ln_g"], wts["ln_b"], mk, mv)
    y_ml, y_cv, y_xa, gates, c, n, mm, conv_out = outs
    return y_ml, y_cv, y_xa, gates, (c, n, mm.reshape(batch, ML_HEADS)), conv_out


def _inproj_kernel(x_ref, g_ref, w_ref, wg_ref, wgt_ref, p_ref, go_ref, gto_ref):
    xn = _rms(x_ref[...], g_ref[...]).astype(BF16)
    for c in range(N_MAIN // D_MODEL):
        p_ref[:, _col(c)] = jnp.dot(xn, w_ref[:, _col(c)], preferred_element_type=F32).astype(BF16)
    go_ref[...] = jnp.dot(xn, wg_ref[...], preferred_element_type=F32)
    gto_ref[...] = lax.dot_general(wgt_ref[...], xn, NT_DIMS, preferred_element_type=F32)


def _inproj(x2d, wts, l, *, batch, seq):
    m = x2d.shape[0]
    p, go, gto = pl.pallas_call(
        _inproj_kernel,
        grid=(1,),
        in_specs=[
            pl.BlockSpec((m, D_MODEL), lambda i: (0, 0)),
            _layer_vec(l),
            _resident((None, D_MODEL, N_MAIN), lambda i: (l, 0, 0)),
            _resident((None, D_MODEL, GATE_PAD), lambda i: (l, 0, 0)),
            _resident((None, GATE_ROWS, D_MODEL), lambda i: (l, 0, 0)),
        ],
        out_specs=[
            pl.BlockSpec((m, N_MAIN), lambda i: (0, 0)),
            pl.BlockSpec((m, GATE_PAD), lambda i: (0, 0)),
            pl.BlockSpec((GATE_ROWS, m), lambda i: (0, 0)),
        ],
        out_shape=[
            jax.ShapeDtypeStruct((m, N_MAIN), BF16),
            jax.ShapeDtypeStruct((m, GATE_PAD), F32),
            jax.ShapeDtypeStruct((GATE_ROWS, m), F32),
        ],
        compiler_params=_params(("arbitrary",)),
        name="inproj",
    )(x2d, wts["g_mix_pre"], wts["w_main"], wts["w_gate"], wts["w_gate_t"])
    return p, go, jnp.transpose(gto.reshape(GATE_ROWS, batch, seq), (1, 0, 2))


def _mlstm_kernel(q_ref, k_ref, v_ref, o_ref, g_ref, gt_ref, brow_ref, bcol_ref, gh_ref, c0_ref, n0_ref, m0_ref,
                  y_ref, c_ref, n_ref, m_ref, *, L):
    @pl.when(pl.program_id(1) == 0)
    def _():
        c_ref[...] = c0_ref[...]
        n_ref[...] = n0_ref[...]
        m_ref[...] = m0_ref[...]

    _mlstm_body((q_ref, 0), (k_ref, 0), (v_ref, 0), (o_ref, 0),
                g_ref[...] + brow_ref[...], gt_ref[...] + bcol_ref[...], gh_ref,
                y_ref, c_ref, n_ref, m_ref, L=L)


def _mlstm(p, go, gto, wts, l, state, *, batch, seq, L):
    nb = seq // L
    m = batch * seq
    rows = lambda b, t: b * nb + t
    c0, n0, m0 = state
    y, c, n, mm = pl.pallas_call(
        functools.partial(_mlstm_kernel, L=L),
        grid=(batch, nb),
        in_specs=[
            pl.BlockSpec((L, D_MODEL), lambda b, t: (rows(b, t), COL_Q)),
            pl.BlockSpec((L, D_MODEL), lambda b, t: (rows(b, t), COL_K)),
            pl.BlockSpec((L, D_MODEL), lambda b, t: (rows(b, t), COL_V)),
            pl.BlockSpec((L, D_MODEL), lambda b, t: (rows(b, t), COL_O)),
            pl.BlockSpec((L, GATE_PAD), lambda b, t: (rows(b, t), 0)),
            pl.BlockSpec((None, GATE_ROWS, L), lambda b, t: (b, 0, t)),
            _layer_vec(l, GATE_PAD),
            pl.BlockSpec((None, GATE_ROWS, 1), lambda b, t: (l, 0, 0)),
            _layer_vec(l),
            pl.BlockSpec((None, None, ML_HEADS, ML_DH, ML_DH), lambda b, t: (l, b, 0, 0, 0)),
            pl.BlockSpec((None, None, ML_HEADS, ML_DH), lambda b, t: (l, b, 0, 0)),
            pl.BlockSpec((None, None, 1, ML_HEADS), lambda b, t: (l, b, 0, 0)),
        ],
        out_specs=[
            pl.BlockSpec((L, D_MODEL), lambda b, t: (rows(b, t), 0)),
            pl.BlockSpec((None, ML_HEADS, ML_DH, ML_DH), lambda b, t: (b, 0, 0, 0)),
            pl.BlockSpec((None, ML_HEADS, ML_DH), lambda b, t: (b, 0, 0)),
            pl.BlockSpec((None, 1, ML_HEADS), lambda b, t: (b, 0, 0)),
        ],
        out_shape=[
            jax.ShapeDtypeStruct((m, D_MODEL), BF16),
            jax.ShapeDtypeStruct((batch, ML_HEADS, ML_DH, ML_DH), F32),
            jax.ShapeDtypeStruct((batch, ML_HEADS, ML_DH), F32),
            jax.ShapeDtypeStruct((batch, 1, ML_HEADS), F32),
        ],
        compiler_params=_params(("parallel", "arbitrary")),
        name="mlstm",
    )(p, p, p, p, go, gto, wts["bias_row"], wts["bias_col"], wts["g_mhead"], c0, n0, m0)
    return y, (c, n, mm.reshape(batch, ML_HEADS))


def _conv_kernel(ga_ref, gb_ref, w_ref, cb_ref, lng_ref, lnb_ref, s0_ref, y_ref, so_ref, ext_ref, out_ref, *, tt):
    @pl.when(pl.program_id(1) == 0)
    def _():
        _conv_init(ext_ref, s0_ref)

    u = ga_ref[...].astype(F32) * _sigmoid(gb_ref[...].astype(F32))
    _conv_body(u, w_ref, cb_ref, lng_ref, lnb_ref, y_ref, so_ref, ext_ref, out_ref, tt=tt)


def _conv(p, wts, l, state, *, batch, seq, tt):
    nt = seq // tt
    m = batch * seq
    rows = lambda b, t: b * nt + t
    return pl.pallas_call(
        functools.partial(_conv_kernel, tt=tt),
        grid=(batch, nt),
        in_specs=[
            pl.BlockSpec((tt, D_MODEL), lambda b, t: (rows(b, t), COL_GA)),
            pl.BlockSpec((tt, D_MODEL), lambda b, t: (rows(b, t), COL_GB)),
            pl.BlockSpec((None, CONV_W, SUBLANES, D_MODEL), lambda b, t: (l, 0, 0, 0)),
            _layer_vec(l), _layer_vec(l), _layer_vec(l),
            pl.BlockSpec((None, None, CONV_BUF, D_MODEL), lambda b, t: (l, b, 0, 0)),
        ],
        out_specs=[pl.BlockSpec((tt, D_MODEL), lambda b, t: (rows(b, t), 0)),
                   pl.BlockSpec((None, CONV_BUF, D_MODEL), lambda b, t: (b, 0, 0))],
        out_shape=[
            jax.ShapeDtypeStruct((m, D_MODEL), BF16),
            jax.ShapeDtypeStruct((batch, CONV_BUF, D_MODEL), F32),
        ],
        scratch_shapes=_conv_scratch(tt),
        compiler_params=_params(("parallel", "arbitrary")),
        name="conv",
    )(p, p, wts["conv_w"], wts["conv_b"], wts["ln_g"], wts["ln_b"], state)


def _xattn_kernel(q_ref, k_ref, v_ref, y_ref, kb_ref, vb_ref):
    @pl.when(pl.program_id(1) == 0)
    def _():
        kb_ref[...] = k_ref[...].astype(BF16)
        vb_ref[...] = v_ref[...].astype(BF16)

    _xattn_body((q_ref, 0), kb_ref, vb_ref, y_ref)


def _xattn(p, mk, mv, l, *, batch, seq, tt):
    nt = seq // tt
    m = batch * seq
    kv_spec = pl.BlockSpec((None, None, MEM_LEN, D_MODEL), lambda b, t: (l, b, 0, 0))
    return pl.pallas_call(
        _xattn_kernel,
        grid=(batch, nt),
        in_specs=[pl.BlockSpec((tt, D_MODEL), lambda b, t: (b * nt + t, COL_XQ)), kv_spec, kv_spec],
        out_specs=pl.BlockSpec((tt, D_MODEL), lambda b, t: (b * nt + t, 0)),
        out_shape=jax.ShapeDtypeStruct((m, D_MODEL), BF16),
        scratch_shapes=[pltpu.VMEM((MEM_LEN, D_MODEL), BF16), pltpu.VMEM((MEM_LEN, D_MODEL), BF16)],
        compiler_params=_params(("parallel", "arbitrary")),
        name="xattn",
    )(p, mk, mv)


def _merge_kernel(x_ref, yml_ref, ycv_ref, yxa_ref, g0_ref, g1_ref, g2_ref, wb_ref, wo_ref, gp_ref, o_ref):
    mixed = None
    for i, (y_ref, gate_ref) in enumerate(((yml_ref, g0_ref), (ycv_ref, g1_ref), (yxa_ref, g2_ref))):
        term = _sigmoid(gate_ref[...].astype(F32)) * jnp.dot(y_ref[...], wb_ref[i], preferred_element_type=F32)
        mixed = term if mixed is None else mixed + term
    z = jnp.dot(mixed.astype(BF16), wo_ref[...], preferred_element_type=F32)
    o_ref[...] = x_ref[...] + _rms(z, gp_ref[...])


def _merge(x2d, y_ml, y_cv, y_xa, gates, gate_col, wts, l, *, tm):
    m = x2d.shape[0]
    tile = lambda c: pl.BlockSpec((tm, D_MODEL), lambda i: (i, c))
    return pl.pallas_call(
        _merge_kernel,
        grid=(m // tm,),
        in_specs=[
            tile(0), tile(0), tile(0), tile(0),
            tile(gate_col), tile(gate_col + 1), tile(gate_col + 2),
            _resident((None, N_BRANCH, D_MODEL, D_MODEL), lambda i: (l, 0, 0, 0)),
            _resident((None, D_MODEL, D_MODEL), lambda i: (l, 0, 0)),
            _layer_vec(l),
        ],
        out_specs=tile(0),
        out_shape=jax.ShapeDtypeStruct((m, D_MODEL), F32),
        compiler_params=_params(("parallel",)),
        name="merge",
    )(x2d, y_ml, y_cv, y_xa, gates, gates, gates, wts["w_branch"], wts["w_out"], wts["g_mix_post"])


FFN_CHUNKS = ((0, 768), (768, 768), (1536, 768), (2304, 512))


def _ffn_kernel(x_ref, gpre_ref, wa_ref, wb_ref, wo_ref, gpost_ref, o_ref, h_ref):
    x = x_ref[...]
    xn = _rms(x, gpre_ref[...]).astype(BF16)
    for start, size in FFN_CHUNKS:
        sl = slice(start, start + size)
        fa = jnp.dot(xn, wa_ref[:, sl], preferred_element_type=F32)
        fb = jnp.dot(xn, wb_ref[:, sl], preferred_element_type=F32)
        h_ref[:, sl] = (fa * _sigmoid(fa) * fb).astype(BF16)
    z = jnp.dot(h_ref[...], wo_ref[...], preferred_element_type=F32)
    o_ref[...] = x + _rms(z, gpost_ref[...])


def _ffn(x2d, wts, l, *, tm):
    m = x2d.shape[0]
    return pl.pallas_call(
        _ffn_kernel,
        grid=(m // tm,),
        in_specs=[
            pl.BlockSpec((tm, D_MODEL), lambda i: (i, 0)),
            _layer_vec(l),
            _resident((None, D_MODEL, D_FF), lambda i: (l, 0, 0)),
            _resident((None, D_MODEL, D_FF), lambda i: (l, 0, 1)),
            _resident((None, D_FF, D_MODEL), lambda i: (l, 0, 0)),
            _layer_vec(l),
        ],
        out_specs=pl.BlockSpec((tm, D_MODEL), lambda i: (i, 0)),
        out_shape=jax.ShapeDtypeStruct((m, D_MODEL), F32),
        scratch_shapes=[pltpu.VMEM((tm, D_FF), BF16)],
        compiler_params=_params(("parallel",)),
        name="ffn",
    )(x2d, wts["g_ffn_pre"], wts["w_ffn_in"], wts["w_ffn_in"], wts["w_ffn_out"], wts["g_ffn_post"])


def _memkv_kernel(x_ref, g_ref, w_ref, k_ref, v_ref):
    xn = _rms(x_ref[...], g_ref[...]).astype(BF16)
    k_ref[...] = jnp.dot(xn, w_ref[:, :D_MODEL], preferred_element_type=F32)
    v_ref[...] = jnp.dot(xn, w_ref[:, D_MODEL:], preferred_element_type=F32)


def _memkv(mem2d, wts, l, *, tm):
    m = mem2d.shape[0]
    out = jax.ShapeDtypeStruct((m, D_MODEL), F32)
    tile = pl.BlockSpec((tm, D_MODEL), lambda i: (i, 0))
    return pl.pallas_call(
        _memkv_kernel,
        grid=(m // tm,),
        in_specs=[tile, _layer_vec(l), _resident((None, D_MODEL, 2 * D_MODEL), lambda i: (l, 0, 0))],
        out_specs=[tile, tile],
        out_shape=[out, out],
        compiler_params=_params(("parallel",)),
        name="memkv",
    )(mem2d, wts["g_mem"], wts["w_mem_kv"])


def _layer_first_chunks(x2d, wts, l, mk, mv, *, batch, seq):
    tiles = _tiles(batch, seq)
    y_ml, y_cv, y_xa, gates, ml_state, conv_out = _branches(x2d, wts, l, mk, mv, batch=batch, seq=seq, tt=tiles["L"])
    x2d = _merge(x2d, y_ml, y_cv, y_xa, gates, 0, wts, l, tm=tiles["tm"])
    x2d = _ffn(x2d, wts, l, tm=tiles["tm"])
    return x2d, ml_state, conv_out


def _layer_later_chunk(x2d, wts, l, ml_state, conv_state, mk, mv, *, batch, seq):
    tiles = _tiles(batch, seq)
    p, go, gto = _inproj(x2d, wts, l, batch=batch, seq=seq)
    y_ml, ml_state = _mlstm(p, go, gto, wts, l, ml_state, batch=batch, seq=seq, L=tiles["L"])
    y_cv, conv_out = _conv(p, wts, l, conv_state, batch=batch, seq=seq, tt=tiles["L"])
    y_xa = _xattn(p, mk, mv, l, batch=batch, seq=seq, tt=tiles["tt_xa"])
    x2d = _merge(x2d, y_ml, y_cv, y_xa, p, COL_GATE, wts, l, tm=tiles["tm"])
    x2d = _ffn(x2d, wts, l, tm=tiles["tm"])
    return x2d, ml_state, conv_out


def kernel(x_prompt, x_sample, mem_prompt, state_mlstm_c, state_mlstm_n, state_mlstm_m, state_conv,
           cache_mem_k, cache_mem_v, norm_mix_pre, norm_mix_post, norm_ffn_pre, norm_ffn_post, w_in, b_gate,
           mlstm_head_norm, conv_w, conv_b, conv_ln_g, conv_ln_b, mem_norm, w_mem_kv, w_branch, w_out,
           w_ffn_in, w_ffn_out):
    depth = w_in.shape[0]
    bp, sp, _ = x_prompt.shape
    bs, ss, _ = x_sample.shape
    gate_lo = 4 * D_MODEL
    gate_hi = gate_lo + N_GATES

    vec = lambda v: v.reshape(depth, 1, -1).astype(F32)
    w_gate_cols = w_in[:, :, gate_lo:gate_hi]
    wts = {
        "w_main": jnp.concatenate([w_in[:, :, :gate_lo].astype(BF16), w_in[:, :, gate_hi:].astype(BF16)], axis=2),
        "w_gate": jnp.pad(w_gate_cols, ((0, 0), (0, 0), (0, GATE_PAD - N_GATES))).astype(BF16),
        "w_gate_t": jnp.pad(jnp.swapaxes(w_gate_cols, 1, 2), ((0, 0), (0, GATE_ROWS - N_GATES), (0, 0))).astype(BF16),
        "bias_row": jnp.pad(b_gate, ((0, 0), (0, GATE_PAD - N_GATES))).reshape(depth, 1, GATE_PAD).astype(F32),
        "bias_col": jnp.pad(b_gate, ((0, 0), (0, GATE_ROWS - N_GATES))).reshape(depth, GATE_ROWS, 1).astype(F32),
        "g_mix_pre": vec(norm_mix_pre), "g_mix_post": vec(norm_mix_post),
        "g_ffn_pre": vec(norm_ffn_pre), "g_ffn_post": vec(norm_ffn_post),
        "g_mhead": vec(mlstm_head_norm), "g_mem": vec(mem_norm),
        "conv_w": jnp.broadcast_to(conv_w.astype(F32)[:, :, None, :], (depth, CONV_W, SUBLANES, D_MODEL)),
        "conv_b": vec(conv_b), "ln_g": vec(conv_ln_g), "ln_b": vec(conv_ln_b),
        "w_branch": w_branch.astype(BF16), "w_out": w_out.astype(BF16),
        "w_ffn_in": w_ffn_in.astype(BF16), "w_ffn_out": w_ffn_out.astype(BF16),
        "w_mem_kv": w_mem_kv.astype(BF16),
    }
    s_state = (state_mlstm_c.astype(F32), state_mlstm_n.astype(F32),
               state_mlstm_m.astype(F32).reshape(depth, bs, 1, ML_HEADS))
    s_conv = state_conv.astype(F32)
    cache_k = cache_mem_k.reshape(depth, bs, MEM_LEN, D_MODEL)
    cache_v = cache_mem_v.reshape(depth, bs, MEM_LEN, D_MODEL)

    yp = x_prompt.reshape(bp * sp, D_MODEL)
    ys = x_sample.reshape(bs * ss, D_MODEL)
    mem2d = mem_prompt.reshape(bp * MEM_LEN, D_MODEL)

    pc, pn, pm, pconv, pmk, pmv = [], [], [], [], [], []
    sc, sn, sm, sconv = [], [], [], []
    for l in range(depth):
        mk_p, mv_p = _memkv(mem2d, wts, l, tm=256)
        yp, (c1, n1, m1), buf1 = _layer_first_chunks(
            yp, wts, l, mk_p.reshape(bp, MEM_LEN, D_MODEL), mv_p.reshape(bp, MEM_LEN, D_MODEL), batch=bp, seq=sp)
        ys, (c2, n2, m2), buf2 = _layer_later_chunk(ys, wts, l, s_state, s_conv, cache_k, cache_v, batch=bs, seq=ss)
        pc.append(c1); pn.append(n1); pm.append(m1); pconv.append(buf1)
        pmk.append(mk_p.reshape(bp, MEM_LEN, XA_HEADS, XA_DH)); pmv.append(mv_p.reshape(bp, MEM_LEN, XA_HEADS, XA_DH))
        sc.append(c2); sn.append(n2); sm.append(m2); sconv.append(buf2)
    return (yp.reshape(bp, sp, D_MODEL), ys.reshape(bs, ss, D_MODEL),
            jnp.stack(pc), jnp.stack(pn), jnp.stack(pm), jnp.stack(pconv), jnp.stack(pmk), jnp.stack(pmv),
            jnp.stack(sc), jnp.stack(sn), jnp.stack(sm), jnp.stack(sconv))
```

```python
import functools

import jax
import jax.numpy as jnp
from jax import lax
from jax.experimental import pallas as pl
from jax.experimental.pallas import tpu as pltpu

F32 = jnp.float32
BF16 = jnp.bfloat16

D_MODEL = 1024
ML_HEADS = 4
ML_DH = 256
XA_HEADS = 4
XA_DH = 256
MEM_LEN = 256
CONV_W = 31
CONV_BUF = CONV_W - 1
D_FF = 2816
RMS_EPS = 1e-6
LN_EPS = 1e-5

N_MAIN = 10 * D_MODEL
COL_Q, COL_K, COL_V, COL_O, COL_GA, COL_GB, COL_XQ, COL_GATE = 0, 1, 2, 3, 4, 5, 6, 7
N_BRANCH = 3
N_GATES = 2 * ML_HEADS
GATE_PAD = 128
GATE_ROWS = 16
FUSED_COLS = (COL_Q, COL_K, COL_V, COL_O, COL_XQ)

V7X_VMEM_LIMIT_BYTES = 56 * 1024 * 1024
SUBLANES = 8
LANES = 128
N_SLABS = D_MODEL // LANES
N_CHUNKS = N_MAIN // D_MODEL
CONV_PAD = 32
CONV_POS_GROUP = 12

NT_DIMS = (((1,), (1,)), ((), ()))
TN_DIMS = (((0,), (0,)), ((), ()))

_sigmoid = jax.nn.sigmoid


def _tiles(batch, seq):
    if seq >= 512:
        return dict(tm=512, L=256, tt_xa=512)
    return dict(tm=batch * seq, L=seq, tt_xa=seq)


def _resident(block_shape, index_map):
    return pl.BlockSpec(block_shape, index_map, pipeline_mode=pl.Buffered(1))


def _layer_vec(l, width=D_MODEL):
    return pl.BlockSpec((None, 1, width), lambda *_: (l, 0, 0))


def _params(semantics):
    return pltpu.CompilerParams(dimension_semantics=semantics, vmem_limit_bytes=V7X_VMEM_LIMIT_BYTES)


def _col(c):
    return slice(c * D_MODEL, (c + 1) * D_MODEL)


def _rms(x, g):
    return x * lax.rsqrt(jnp.mean(x * x, axis=-1, keepdims=True) + RMS_EPS) * g


def _mlstm_body(q_src, k_src, v_src, o_src, gates_c, gates_r, gh_ref, y_ref, c_ref, n_ref, m_ref, *, L):
    row = lax.broadcasted_iota(jnp.int32, (L, L), 0)
    col = lax.broadcasted_iota(jnp.int32, (L, L), 1)
    causal = col <= row
    tril = causal.astype(F32)
    triu = (row <= col).astype(F32)
    cum_c = jnp.dot(tril, jax.nn.log_sigmoid(gates_c), precision=lax.Precision.HIGHEST,
                    preferred_element_type=F32)
    cum_r = jnp.dot(jax.nn.log_sigmoid(gates_r), triu, precision=lax.Precision.HIGHEST,
                    preferred_element_type=F32)

    def head_cols(src, h):
        ref, first = src
        return ref[:, first + h * ML_DH:first + (h + 1) * ML_DH]

    for h in range(ML_HEADS):
        hs = slice(h * ML_DH, (h + 1) * ML_DH)
        ig_c = gates_c[:, h:h + 1]
        b_c = cum_c[:, ML_HEADS + h:ML_HEADS + h + 1]
        src_r = cum_r[ML_HEADS + h:ML_HEADS + h + 1, :] - gates_r[h:h + 1, :]
        m_prev = m_ref[:, h:h + 1]

        d = jnp.where(causal, b_c - src_r, -jnp.inf)
        a = b_c + m_prev
        m_row = jnp.maximum(a, jnp.max(d, axis=1, keepdims=True))

        q = head_cols(q_src, h)
        ks = head_cols(k_src, h) * jnp.asarray(ML_DH ** -0.5, BF16)
        v = head_cols(v_src, h)
        s = lax.dot_general(q, ks, NT_DIMS, preferred_element_type=F32) * jnp.exp(d - m_row)
        w_inter = jnp.exp(a - m_row)

        c = c_ref[h]
        n = n_ref[h:h + 1, :]
        qc = lax.dot_general(q, c.astype(BF16), NT_DIMS, preferred_element_type=F32)
        num = w_inter * qc + jnp.dot(s.astype(BF16), v, preferred_element_type=F32)
        qn = jnp.sum(q.astype(F32) * n, axis=1, keepdims=True)
        den = w_inter * qn + jnp.sum(s, axis=1, keepdims=True)
        hh = num * (1.0 / jnp.maximum(jnp.abs(den), jnp.exp(-m_row)))

        b_last = b_c[L - 1:L, :]
        g_c = b_last - b_c + ig_c
        m_new = jnp.maximum(b_last + m_prev, jnp.max(g_c, axis=0, keepdims=True))
        w_old = jnp.exp(b_last + m_prev - m_new)
        w_s = jnp.exp(g_c - m_new)
        wv = (w_s * v.astype(F32)).astype(BF16)
        c_ref[h] = w_old * c + lax.dot_general(wv, ks, TN_DIMS, preferred_element_type=F32)
        n_ref[h:h + 1, :] = w_old * n + jnp.sum(w_s * ks.astype(F32), axis=0, keepdims=True)
        m_ref[:, h:h + 1] = m_new

        hn = hh * lax.rsqrt(jnp.mean(hh * hh, axis=1, keepdims=True) + RMS_EPS) * gh_ref[:, hs]
        y_ref[:, hs] = (hn * _sigmoid(head_cols(o_src, h).astype(F32))).astype(BF16)


def _conv_init(ext_ref, s0_ref):
    ext_ref[...] = jnp.zeros_like(ext_ref)
    if s0_ref is not None:
        for c in range(N_SLABS):
            ext_ref[c, CONV_PAD - CONV_BUF:CONV_PAD, :] = s0_ref[:, c * LANES:(c + 1) * LANES]


def _conv_store(u, ext_ref, *, tt):
    for c in range(N_SLABS):
        ext_ref[c, CONV_PAD:CONV_PAD + tt, :] = u[:, c * LANES:(c + 1) * LANES]


def _conv_taps(c, w_ref, cb_ref, ext_ref, out_ref, *, tt):
    seg = (tt + CONV_PAD) // SUBLANES
    bias = cb_ref[c]
    for p0 in range(0, seg, CONV_POS_GROUP):
        p1 = min(p0 + CONV_POS_GROUP, seg)
        acc = [None] * (p1 - p0)
        for q in range(p0, p1 + CONV_W - 1):
            z = ext_ref[c, pl.ds(q, SUBLANES, stride=seg), :]
            for p in range(max(p0, q - CONV_W + 1), min(p1, q + 1)):
                term = z * w_ref[q - p, c]
                acc[p - p0] = term if acc[p - p0] is None else acc[p - p0] + term
        for p in range(p0, p1):
            out_ref[c, pl.ds(p + CONV_BUF, SUBLANES, stride=seg), :] = acc[p - p0] + bias


def _conv_finish(lng_ref, lnb_ref, y_ref, so_ref, ext_ref, out_ref, *, tt):
    lo = CONV_PAD - CONV_BUF
    for c in range(N_SLABS):
        so_ref[:, c * LANES:(c + 1) * LANES] = ext_ref[c, tt + lo:tt + CONV_PAD, :]
        ext_ref[c, 0:CONV_PAD, :] = ext_ref[c, tt:tt + CONV_PAD, :]

    x = jnp.concatenate([out_ref[c, CONV_PAD:CONV_PAD + tt, :] for c in range(N_SLABS)], axis=1)
    mu = jnp.mean(x, axis=-1, keepdims=True)
    xc = x - mu
    yn = xc * lax.rsqrt(jnp.mean(xc * xc, axis=-1, keepdims=True) + LN_EPS) * lng_ref[...] + lnb_ref[...]
    y_ref[...] = (yn * _sigmoid(yn)).astype(BF16)


def _conv_scratch(tt):
    slab = pltpu.VMEM((N_SLABS, tt + 2 * CONV_PAD, LANES), F32)
    return [slab, slab]


def _xattn_body(q_src, kb_ref, vb_ref, y_ref):
    q_ref, first = q_src
    for h in range(XA_HEADS):
        hs = slice(h * XA_DH, (h + 1) * XA_DH)
        q = q_ref[:, first + h * XA_DH:first + (h + 1) * XA_DH]
        s = lax.dot_general(q, kb_ref[:, hs], NT_DIMS, preferred_element_type=F32)
        s = s * (XA_DH ** -0.5)
        e = jnp.exp(s - jnp.max(s, axis=-1, keepdims=True))
        p = e * (1.0 / jnp.sum(e, axis=-1, keepdims=True))
        y_ref[:, hs] = jnp.dot(p.astype(BF16), vb_ref[:, hs], preferred_element_type=F32).astype(BF16)


def _branches_kernel(x_ref, gpre_ref, w_ref, wg_ref, wgt_ref, brow_ref, bcol_ref, gh_ref,
                     cw_ref, cb_ref, lng_ref, lnb_ref, mk_ref, mv_ref,
                     yml_ref, ycv_ref, yxa_ref, gate_ref, c_ref, n_ref, m_ref, so_ref,
                     xn_scr, p_scr, ext_ref, out_ref, kb_ref, vb_ref, *, tt):
    @pl.when(pl.program_id(1) == 0)
    def _():
        c_ref[...] = jnp.zeros_like(c_ref)
        n_ref[...] = jnp.zeros_like(n_ref)
        m_ref[...] = jnp.zeros_like(m_ref)
        _conv_init(ext_ref, None)
        kb_ref[...] = mk_ref[...].astype(BF16)
        vb_ref[...] = mv_ref[...].astype(BF16)

    xn_scr[...] = _rms(x_ref[...], gpre_ref[...]).astype(BF16)
    proj = lambda c: jnp.dot(xn_scr[...], w_ref[c], preferred_element_type=F32)
    _conv_store(proj(COL_GA) * _sigmoid(proj(COL_GB)), ext_ref, tt=tt)

    for j, c in enumerate(FUSED_COLS):
        p_scr[j] = proj(c).astype(BF16)
    for c in range(N_SLABS):
        _conv_taps(c, cw_ref, cb_ref, ext_ref, out_ref, tt=tt)
    for j in range(N_BRANCH):
        gate_ref[:, _col(j)] = proj(COL_GATE + j).astype(BF16)
    xn = xn_scr[...]
    gates_c = jnp.dot(xn, wg_ref[...], preferred_element_type=F32) + brow_ref[...]
    gates_r = lax.dot_general(wgt_ref[...], xn, NT_DIMS, preferred_element_type=F32) + bcol_ref[...]

    _conv_finish(lng_ref, lnb_ref, ycv_ref, so_ref, ext_ref, out_ref, tt=tt)
    src = lambda c: (p_scr.at[FUSED_COLS.index(c)], 0)
    _mlstm_body(src(COL_Q), src(COL_K), src(COL_V), src(COL_O), gates_c, gates_r, gh_ref,
                yml_ref, c_ref, n_ref, m_ref, L=tt)
    _xattn_body(src(COL_XQ), kb_ref, vb_ref, yxa_ref)


def _branches(x2d, wts, l, mk, mv, *, batch, seq, tt):
    nt = seq // tt
    m = batch * seq
    rows = lambda b, t: (b * nt + t, 0)
    tile = lambda width: pl.BlockSpec((tt, width), rows)
    kv_spec = pl.BlockSpec((None, MEM_LEN, D_MODEL), lambda b, t: (b, 0, 0))
    y = jax.ShapeDtypeStruct((m, D_MODEL), BF16)
    outs = pl.pallas_call(
        functools.partial(_branches_kernel, tt=tt),
        grid=(batch, nt),
        in_specs=[
            tile(D_MODEL), _layer_vec(l),
            _resident((None, N_CHUNKS, D_MODEL, D_MODEL), lambda b, t: (l, 0, 0, 0)),
            _resident((None, D_MODEL, GATE_PAD), lambda b, t: (l, 0, 0)),
            _resident((None, GATE_ROWS, D_MODEL), lambda b, t: (l, 0, 0)),
            _layer_vec(l, GATE_PAD),
            pl.BlockSpec((None, GATE_ROWS, 1), lambda b, t: (l, 0, 0)),
            _layer_vec(l),
            pl.BlockSpec((None, CONV_W, N_SLABS, SUBLANES, LANES), lambda b, t: (l, 0, 0, 0, 0)),
            pl.BlockSpec((None, N_SLABS, SUBLANES, LANES), lambda b, t: (l, 0, 0, 0)),
            _layer_vec(l), _layer_vec(l),
            kv_spec, kv_spec,
        ],
        out_specs=[
            tile(D_MODEL), tile(D_MODEL), tile(D_MODEL), tile(N_BRANCH * D_MODEL),
            pl.BlockSpec((None, ML_HEADS, ML_DH, ML_DH), lambda b, t: (b, 0, 0, 0)),
            pl.BlockSpec((None, ML_HEADS, ML_DH), lambda b, t: (b, 0, 0)),
            pl.BlockSpec((None, 1, ML_HEADS), lambda b, t: (b, 0, 0)),
            pl.BlockSpec((None, CONV_BUF, D_MODEL), lambda b, t: (b, 0, 0)),
        ],
        out_shape=[
            y, y, y, jax.ShapeDtypeStruct((m, N_BRANCH * D_MODEL), BF16),
            jax.ShapeDtypeStruct((batch, ML_HEADS, ML_DH, ML_DH), F32),
            jax.ShapeDtypeStruct((batch, ML_HEADS, ML_DH), F32),
            jax.ShapeDtypeStruct((batch, 1, ML_HEADS), F32),
            jax.ShapeDtypeStruct((batch, CONV_BUF, D_MODEL), F32),
        ],
        scratch_shapes=[pltpu.VMEM((tt, D_MODEL), BF16), pltpu.VMEM((len(FUSED_COLS), tt, D_MODEL), BF16)]
                       + _conv_scratch(tt)
                       + [pltpu.VMEM((MEM_LEN, D_MODEL), BF16), pltpu.VMEM((MEM_LEN, D_MODEL), BF16)],
        compiler_params=_params(("parallel", "arbitrary")),
        name="branches",
    )(x2d, wts["g_mix_pre"], wts["w_main"], wts["w_gate"], wts["w_gate_t"], wts["bias_row"], wts["bias_col"],
      wts["g_mhead"], wts["conv_w"], wts["conv_b"], wts["ln_g"], wts["ln_b"], mk, mv)
    y_ml, y_cv, y_xa, gates, c, n, mm, conv_out = outs
    return y_ml, y_cv, y_xa, gates, (c, n, mm.reshape(batch, ML_HEADS)), conv_out


def _inproj_kernel(x_ref, g_ref, w_ref, wg_ref, wgt_ref, p_ref, go_ref, gto_ref):
    xn = _rms(x_ref[...], g_ref[...]).astype(BF16)
    for c in range(N_MAIN // D_MODEL):
        p_ref[:, _col(c)] = jnp.dot(xn, w_ref[c], preferred_element_type=F32).astype(BF16)
    go_ref[...] = jnp.dot(xn, wg_ref[...], preferred_element_type=F32)
    gto_ref[...] = lax.dot_general(wgt_ref[...], xn, NT_DIMS, preferred_element_type=F32)


def _inproj(x2d, wts, l, *, batch, seq):
    m = x2d.shape[0]
    p, go, gto = pl.pallas_call(
        _inproj_kernel,
        grid=(1,),
        in_specs=[
            pl.BlockSpec((m, D_MODEL), lambda i: (0, 0)),
            _layer_vec(l),
            _resident((None, N_CHUNKS, D_MODEL, D_MODEL), lambda i: (l, 0, 0, 0)),
            _resident((None, D_MODEL, GATE_PAD), lambda i: (l, 0, 0)),
            _resident((None, GATE_ROWS, D_MODEL), lambda i: (l, 0, 0)),
        ],
        out_specs=[
            pl.BlockSpec((m, N_MAIN), lambda i: (0, 0)),
            pl.BlockSpec((m, GATE_PAD), lambda i: (0, 0)),
            pl.BlockSpec((GATE_ROWS, m), lambda i: (0, 0)),
        ],
        out_shape=[
            jax.ShapeDtypeStruct((m, N_MAIN), BF16),
            jax.ShapeDtypeStruct((m, GATE_PAD), F32),
            jax.ShapeDtypeStruct((GATE_ROWS, m), F32),
        ],
        compiler_params=_params(("arbitrary",)),
        name="inproj",
    )(x2d, wts["g_mix_pre"], wts["w_main"], wts["w_gate"], wts["w_gate_t"])
    return p, go, jnp.transpose(gto.reshape(GATE_ROWS, batch, seq), (1, 0, 2))


def _mlstm_kernel(q_ref, k_ref, v_ref, o_ref, g_ref, gt_ref, brow_ref, bcol_ref, gh_ref, c0_ref, n0_ref, m0_ref,
                  y_ref, c_ref, n_ref, m_ref, *, L):
    @pl.when(pl.program_id(1) == 0)
    def _():
        c_ref[...] = c0_ref[...]
        n_ref[...] = n0_ref[...]
        m_ref[...] = m0_ref[...]

    _mlstm_body((q_ref, 0), (k_ref, 0), (v_ref, 0), (o_ref, 0),
                g_ref[...] + brow_ref[...], gt_ref[...] + bcol_ref[...], gh_ref,
                y_ref, c_ref, n_ref, m_ref, L=L)


def _mlstm(p, go, gto, wts, l, state, *, batch, seq, L):
    nb = seq // L
    m = batch * seq
    rows = lambda b, t: b * nb + t
    c0, n0, m0 = state
    y, c, n, mm = pl.pallas_call(
        functools.partial(_mlstm_kernel, L=L),
        grid=(batch, nb),
        in_specs=[
            pl.BlockSpec((L, D_MODEL), lambda b, t: (rows(b, t), COL_Q)),
            pl.BlockSpec((L, D_MODEL), lambda b, t: (rows(b, t), COL_K)),
            pl.BlockSpec((L, D_MODEL), lambda b, t: (rows(b, t), COL_V)),
            pl.BlockSpec((L, D_MODEL), lambda b, t: (rows(b, t), COL_O)),
            pl.BlockSpec((L, GATE_PAD), lambda b, t: (rows(b, t), 0)),
            pl.BlockSpec((None, GATE_ROWS, L), lambda b, t: (b, 0, t)),
            _layer_vec(l, GATE_PAD),
            pl.BlockSpec((None, GATE_ROWS, 1), lambda b, t: (l, 0, 0)),
            _layer_vec(l),
            pl.BlockSpec((None, None, ML_HEADS, ML_DH, ML_DH), lambda b, t: (l, b, 0, 0, 0)),
            pl.BlockSpec((None, None, ML_HEADS, ML_DH), lambda b, t: (l, b, 0, 0)),
            pl.BlockSpec((None, None, 1, ML_HEADS), lambda b, t: (l, b, 0, 0)),
        ],
        out_specs=[
            pl.BlockSpec((L, D_MODEL), lambda b, t: (rows(b, t), 0)),
            pl.BlockSpec((None, ML_HEADS, ML_DH, ML_DH), lambda b, t: (b, 0, 0, 0)),
            pl.BlockSpec((None, ML_HEADS, ML_DH), lambda b, t: (b, 0, 0)),
            pl.BlockSpec((None, 1, ML_HEADS), lambda b, t: (b, 0, 0)),
        ],
        out_shape=[
            jax.ShapeDtypeStruct((m, D_MODEL), BF16),
            jax.ShapeDtypeStruct((batch, ML_HEADS, ML_DH, ML_DH), F32),
            jax.ShapeDtypeStruct((batch, ML_HEADS, ML_DH), F32),
            jax.ShapeDtypeStruct((batch, 1, ML_HEADS), F32),
        ],
        compiler_params=_params(("parallel", "arbitrary")),
        name="mlstm",
    )(p, p, p, p, go, gto, wts["bias_row"], wts["bias_col"], wts["g_mhead"], c0, n0, m0)
    return y, (c, n, mm.reshape(batch, ML_HEADS))


def _conv_kernel(ga_ref, gb_ref, w_ref, cb_ref, lng_ref, lnb_ref, s0_ref, y_ref, so_ref, ext_ref, out_ref, *, tt):
    @pl.when(pl.program_id(1) == 0)
    def _():
        _conv_init(ext_ref, s0_ref)

    u = ga_ref[...].astype(F32) * _sigmoid(gb_ref[...].astype(F32))
    _conv_store(u, ext_ref, tt=tt)
    for c in range(N_SLABS):
        _conv_taps(c, w_ref, cb_ref, ext_ref, out_ref, tt=tt)
    _conv_finish(lng_ref, lnb_ref, y_ref, so_ref, ext_ref, out_ref, tt=tt)


def _conv(p, wts, l, state, *, batch, seq, tt):
    nt = seq // tt
    m = batch * seq
    rows = lambda b, t: b * nt + t
    return pl.pallas_call(
        functools.partial(_conv_kernel, tt=tt),
        grid=(batch, nt),
        in_specs=[
            pl.BlockSpec((tt, D_MODEL), lambda b, t: (rows(b, t), COL_GA)),
            pl.BlockSpec((tt, D_MODEL), lambda b, t: (rows(b, t), COL_GB)),
            pl.BlockSpec((None, CONV_W, N_SLABS, SUBLANES, LANES), lambda b, t: (l, 0, 0, 0, 0)),
            pl.BlockSpec((None, N_SLABS, SUBLANES, LANES), lambda b, t: (l, 0, 0, 0)),
            _layer_vec(l), _layer_vec(l),
            pl.BlockSpec((None, None, CONV_BUF, D_MODEL), lambda b, t: (l, b, 0, 0)),
        ],
        out_specs=[pl.BlockSpec((tt, D_MODEL), lambda b, t: (rows(b, t), 0)),
                   pl.BlockSpec((None, CONV_BUF, D_MODEL), lambda b, t: (b, 0, 0))],
        out_shape=[
            jax.ShapeDtypeStruct((m, D_MODEL), BF16),
            jax.ShapeDtypeStruct((batch, CONV_BUF, D_MODEL), F32),
        ],
        scratch_shapes=_conv_scratch(tt),
        compiler_params=_params(("parallel", "arbitrary")),
        name="conv",
    )(p, p, wts["conv_w"], wts["conv_b"], wts["ln_g"], wts["ln_b"], state)


def _xattn_kernel(q_ref, k_ref, v_ref, y_ref, kb_ref, vb_ref):
    @pl.when(pl.program_id(1) == 0)
    def _():
        kb_ref[...] = k_ref[...].astype(BF16)
        vb_ref[...] = v_ref[...].astype(BF16)

    _xattn_body((q_ref, 0), kb_ref, vb_ref, y_ref)


def _xattn(p, mk, mv, l, *, batch, seq, tt):
    nt = seq // tt
    m = batch * seq
    kv_spec = pl.BlockSpec((None, None, MEM_LEN, D_MODEL), lambda b, t: (l, b, 0, 0))
    return pl.pallas_call(
        _xattn_kernel,
        grid=(batch, nt),
        in_specs=[pl.BlockSpec((tt, D_MODEL), lambda b, t: (b * nt + t, COL_XQ)), kv_spec, kv_spec],
        out_specs=pl.BlockSpec((tt, D_MODEL), lambda b, t: (b * nt + t, 0)),
        out_shape=jax.ShapeDtypeStruct((m, D_MODEL), BF16),
        scratch_shapes=[pltpu.VMEM((MEM_LEN, D_MODEL), BF16), pltpu.VMEM((MEM_LEN, D_MODEL), BF16)],
        compiler_params=_params(("parallel", "arbitrary")),
        name="xattn",
    )(p, mk, mv)


def _merge_kernel(x_ref, yml_ref, ycv_ref, yxa_ref, g0_ref, g1_ref, g2_ref, wb_ref, wo_ref, gp_ref, o_ref):
    mixed = None
    for i, (y_ref, gate_ref) in enumerate(((yml_ref, g0_ref), (ycv_ref, g1_ref), (yxa_ref, g2_ref))):
        term = _sigmoid(gate_ref[...].astype(F32)) * jnp.dot(y_ref[...], wb_ref[i], preferred_element_type=F32)
        mixed = term if mixed is None else mixed + term
    z = jnp.dot(mixed.astype(BF16), wo_ref[...], preferred_element_type=F32)
    o_ref[...] = x_ref[...] + _rms(z, gp_ref[...])


def _merge(x2d, y_ml, y_cv, y_xa, gates, gate_col, wts, l, *, tm):
    m = x2d.shape[0]
    tile = lambda c: pl.BlockSpec((tm, D_MODEL), lambda i: (i, c))
    return pl.pallas_call(
        _merge_kernel,
        grid=(m // tm,),
        in_specs=[
            tile(0), tile(0), tile(0), tile(0),
            tile(gate_col), tile(gate_col + 1), tile(gate_col + 2),
            _resident((None, N_BRANCH, D_MODEL, D_MODEL), lambda i: (l, 0, 0, 0)),
            _resident((None, D_MODEL, D_MODEL), lambda i: (l, 0, 0)),
            _layer_vec(l),
        ],
        out_specs=tile(0),
        out_shape=jax.ShapeDtypeStruct((m, D_MODEL), F32),
        compiler_params=_params(("parallel",)),
        name="merge",
    )(x2d, y_ml, y_cv, y_xa, gates, gates, gates, wts["w_branch"], wts["w_out"], wts["g_mix_post"])


FFN_CHUNKS = ((0, 768), (768, 768), (1536, 768), (2304, 512))


def _ffn_kernel(x_ref, gpre_ref, wa_ref, wb_ref, wo_ref, gpost_ref, o_ref, h_ref):
    x = x_ref[...]
    xn = _rms(x, gpre_ref[...]).astype(BF16)
    for start, size in FFN_CHUNKS:
        sl = slice(start, start + size)
        fa = jnp.dot(xn, wa_ref[:, sl], preferred_element_type=F32)
        fb = jnp.dot(xn, wb_ref[:, sl], preferred_element_type=F32)
        h_ref[:, sl] = (fa * _sigmoid(fa) * fb).astype(BF16)
    z = jnp.dot(h_ref[...], wo_ref[...], preferred_element_type=F32)
    o_ref[...] = x + _rms(z, gpost_ref[...])


def _ffn(x2d, wts, l, *, tm):
    m = x2d.shape[0]
    return pl.pallas_call(
        _ffn_kernel,
        grid=(m // tm,),
        in_specs=[
            pl.BlockSpec((tm, D_MODEL), lambda i: (i, 0)),
            _layer_vec(l),
            _resident((None, D_MODEL, D_FF), lambda i: (l, 0, 0)),
            _resident((None, D_MODEL, D_FF), lambda i: (l, 0, 1)),
            _resident((None, D_FF, D_MODEL), lambda i: (l, 0, 0)),
            _layer_vec(l),
        ],
        out_specs=pl.BlockSpec((tm, D_MODEL), lambda i: (i, 0)),
        out_shape=jax.ShapeDtypeStruct((m, D_MODEL), F32),
        scratch_shapes=[pltpu.VMEM((tm, D_FF), BF16)],
        compiler_params=_params(("parallel",)),
        name="ffn",
    )(x2d, wts["g_ffn_pre"], wts["w_ffn_in"], wts["w_ffn_in"], wts["w_ffn_out"], wts["g_ffn_post"])


def _memkv_kernel(x_ref, g_ref, w_ref, k_ref, v_ref):
    xn = _rms(x_ref[...], g_ref[...]).astype(BF16)
    k_ref[...] = jnp.dot(xn, w_ref[:, :D_MODEL], preferred_element_type=F32)
    v_ref[...] = jnp.dot(xn, w_ref[:, D_MODEL:], preferred_element_type=F32)


def _memkv(mem2d, wts, l, *, tm):
    m = mem2d.shape[0]
    out = jax.ShapeDtypeStruct((m, D_MODEL), F32)
    tile = pl.BlockSpec((tm, D_MODEL), lambda i: (i, 0))
    return pl.pallas_call(
        _memkv_kernel,
        grid=(m // tm,),
        in_specs=[tile, _layer_vec(l), _resident((None, D_MODEL, 2 * D_MODEL), lambda i: (l, 0, 0))],
        out_specs=[tile, tile],
        out_shape=[out, out],
        compiler_params=_params(("parallel",)),
        name="memkv",
    )(mem2d, wts["g_mem"], wts["w_mem_kv"])


def _layer_first_chunks(x2d, wts, l, mk, mv, *, batch, seq):
    tiles = _tiles(batch, seq)
    y_ml, y_cv, y_xa, gates, ml_state, conv_out = _branches(x2d, wts, l, mk, mv, batch=batch, seq=seq, tt=tiles["L"])
    x2d = _merge(x2d, y_ml, y_cv, y_xa, gates, 0, wts, l, tm=tiles["tm"])
    x2d = _ffn(x2d, wts, l, tm=tiles["tm"])
    return x2d, ml_state, conv_out


def _layer_later_chunk(x2d, wts, l, ml_state, conv_state, mk, mv, *, batch, seq):
    tiles = _tiles(batch, seq)
    p, go, gto = _inproj(x2d, wts, l, batch=batch, seq=seq)
    y_ml, ml_state = _mlstm(p, go, gto, wts, l, ml_state, batch=batch, seq=seq, L=tiles["L"])
    y_cv, conv_out = _conv(p, wts, l, conv_state, batch=batch, seq=seq, tt=tiles["L"])
    y_xa = _xattn(p, mk, mv, l, batch=batch, seq=seq, tt=tiles["tt_xa"])
    x2d = _merge(x2d, y_ml, y_cv, y_xa, p, COL_GATE, wts, l, tm=tiles["tm"])
    x2d = _ffn(x2d, wts, l, tm=tiles["tm"])
    return x2d, ml_state, conv_out


def kernel(x_prompt, x_sample, mem_prompt, state_mlstm_c, state_mlstm_n, state_mlstm_m, state_conv,
           cache_mem_k, cache_mem_v, norm_mix_pre, norm_mix_post, norm_ffn_pre, norm_ffn_post, w_in, b_gate,
           mlstm_head_norm, conv_w, conv_b, conv_ln_g, conv_ln_b, mem_norm, w_mem_kv, w_branch, w_out,
           w_ffn_in, w_ffn_out):
    depth = w_in.shape[0]
    bp, sp, _ = x_prompt.shape
    bs, ss, _ = x_sample.shape
    gate_lo = 4 * D_MODEL
    gate_hi = gate_lo + N_GATES
    chunk_starts = [c * D_MODEL + (N_GATES if c * D_MODEL >= gate_lo else 0) for c in range(N_CHUNKS)]

    vec = lambda v: v.reshape(depth, 1, -1).astype(F32)
    w_gate_cols = w_in[:, :, gate_lo:gate_hi]
    wts = {
        "w_main": jnp.stack([w_in[:, :, a:a + D_MODEL].astype(BF16) for a in chunk_starts], axis=1),
        "w_gate": jnp.pad(w_gate_cols, ((0, 0), (0, 0), (0, GATE_PAD - N_GATES))).astype(BF16),
        "w_gate_t": jnp.pad(jnp.swapaxes(w_gate_cols, 1, 2), ((0, 0), (0, GATE_ROWS - N_GATES), (0, 0))).astype(BF16),
        "bias_row": jnp.pad(b_gate, ((0, 0), (0, GATE_PAD - N_GATES))).reshape(depth, 1, GATE_PAD).astype(F32),
        "bias_col": jnp.pad(b_gate, ((0, 0), (0, GATE_ROWS - N_GATES))).reshape(depth, GATE_ROWS, 1).astype(F32),
        "g_mix_pre": vec(norm_mix_pre), "g_mix_post": vec(norm_mix_post),
        "g_ffn_pre": vec(norm_ffn_pre), "g_ffn_post": vec(norm_ffn_post),
        "g_mhead": vec(mlstm_head_norm), "g_mem": vec(mem_norm),
        "conv_w": jnp.broadcast_to(conv_w.astype(F32).reshape(depth, CONV_W, N_SLABS, 1, LANES),
                                   (depth, CONV_W, N_SLABS, SUBLANES, LANES)),
        "conv_b": jnp.broadcast_to(conv_b.astype(F32).reshape(depth, N_SLABS, 1, LANES),
                                   (depth, N_SLABS, SUBLANES, LANES)),
        "ln_g": vec(conv_ln_g), "ln_b": vec(conv_ln_b),
        "w_branch": w_branch.astype(BF16), "w_out": w_out.astype(BF16),
        "w_ffn_in": w_ffn_in.astype(BF16), "w_ffn_out": w_ffn_out.astype(BF16),
        "w_mem_kv": w_mem_kv.astype(BF16),
    }
    s_state = (state_mlstm_c.astype(F32), state_mlstm_n.astype(F32),
               state_mlstm_m.astype(F32).reshape(depth, bs, 1, ML_HEADS))
    s_conv = state_conv.astype(F32)
    cache_k = cache_mem_k.reshape(depth, bs, MEM_LEN, D_MODEL)
    cache_v = cache_mem_v.reshape(depth, bs, MEM_LEN, D_MODEL)

    yp = x_prompt.reshape(bp * sp, D_MODEL)
    ys = x_sample.reshape(bs * ss, D_MODEL)
    mem2d = mem_prompt.reshape(bp * MEM_LEN, D_MODEL)

    pc, pn, pm, pconv, pmk, pmv = [], [], [], [], [], []
    sc, sn, sm, sconv = [], [], [], []
    for l in range(depth):
        mk_p, mv_p = _memkv(mem2d, wts, l, tm=256)
        yp, (c1, n1, m1), buf1 = _layer_first_chunks(
            yp, wts, l, mk_p.reshape(bp, MEM_LEN, D_MODEL), mv_p.reshape(bp, MEM_LEN, D_MODEL), batch=bp, seq=sp)
        ys, (c2, n2, m2), buf2 = _layer_later_chunk(ys, wts, l, s_state, s_conv, cache_k, cache_v, batch=bs, seq=ss)
        pc.append(c1); pn.append(n1); pm.append(m1); pconv.append(buf1)
        pmk.append(mk_p.reshape(bp, MEM_LEN, XA_HEADS, XA_DH)); pmv.append(mv_p.reshape(bp, MEM_LEN, XA_HEADS, XA_DH))
        sc.append(c2); sn.append(n2); sm.append(m2); sconv.append(buf2)
    return (yp.reshape(bp, sp, D_MODEL), ys.reshape(bs, ss, D_MODEL),
            jnp.stack(pc), jnp.stack(pn), jnp.stack(pm), jnp.stack(pconv), jnp.stack(pmk), jnp.stack(pmv),
            jnp.stack(sc), jnp.stack(sn), jnp.stack(sm), jnp.stack(sconv))
```

```python
import functools

import jax
import jax.numpy as jnp
from jax import lax
from jax.experimental import pallas as pl
from jax.experimental.pallas import tpu as pltpu

F32 = jnp.float32
BF16 = jnp.bfloat16

D_MODEL = 1024
ML_HEADS = 4
ML_DH = 256
XA_HEADS = 4
XA_DH = 256
MEM_LEN = 256
CONV_W = 31
CONV_BUF = CONV_W - 1
D_FF = 2816
RMS_EPS = 1e-6
LN_EPS = 1e-5

N_MAIN = 10 * D_MODEL
COL_Q, COL_K, COL_V, COL_O, COL_GA, COL_GB, COL_XQ, COL_GATE = 0, 1, 2, 3, 4, 5, 6, 7
N_BRANCH = 3
N_GATES = 2 * ML_HEADS
GATE_PAD = 128
GATE_ROWS = 16
FUSED_COLS = (COL_Q, COL_K, COL_V, COL_O, COL_XQ)

V7X_VMEM_LIMIT_BYTES = 56 * 1024 * 1024
SUBLANES = 8
LANES = 128
N_SLABS = D_MODEL // LANES
N_CHUNKS = N_MAIN // D_MODEL
CONV_PAD = 32
CONV_POS_GROUP = 12

NT_DIMS = (((1,), (1,)), ((), ()))
TN_DIMS = (((0,), (0,)), ((), ()))

_sigmoid = jax.nn.sigmoid


def _tiles(batch, seq):
    if seq >= 512:
        return dict(tm=512, L=256, tt_xa=512)
    return dict(tm=batch * seq, L=seq, tt_xa=seq)


def _resident(block_shape, index_map):
    return pl.BlockSpec(block_shape, index_map, pipeline_mode=pl.Buffered(1))


def _layer_vec(l, width=D_MODEL):
    return pl.BlockSpec((None, 1, width), lambda *_: (l, 0, 0))


def _params(semantics):
    return pltpu.CompilerParams(dimension_semantics=semantics, vmem_limit_bytes=V7X_VMEM_LIMIT_BYTES)


def _col(c):
    return slice(c * D_MODEL, (c + 1) * D_MODEL)


def _rms(x, g):
    return x * lax.rsqrt(jnp.mean(x * x, axis=-1, keepdims=True) + RMS_EPS) * g


def _mlstm_body(q_src, k_src, v_src, o_src, gates_c, gates_r, gh_ref, y_ref, c_ref, n_ref, m_ref, *, L):
    row = lax.broadcasted_iota(jnp.int32, (L, L), 0)
    col = lax.broadcasted_iota(jnp.int32, (L, L), 1)
    causal = col <= row
    tril = causal.astype(F32)
    triu = (row <= col).astype(F32)
    cum_c = jnp.dot(tril, jax.nn.log_sigmoid(gates_c), precision=lax.Precision.HIGHEST,
                    preferred_element_type=F32)
    cum_r = jnp.dot(jax.nn.log_sigmoid(gates_r), triu, precision=lax.Precision.HIGHEST,
                    preferred_element_type=F32)

    def head_cols(src, h):
        ref, first = src
        return ref[:, first + h * ML_DH:first + (h + 1) * ML_DH]

    H = range(ML_HEADS)
    hs = [slice(h * ML_DH, (h + 1) * ML_DH) for h in H]
    ig_c = [gates_c[:, h:h + 1] for h in H]
    b_c = [cum_c[:, ML_HEADS + h:ML_HEADS + h + 1] for h in H]
    src_r = [cum_r[ML_HEADS + h:ML_HEADS + h + 1, :] - gates_r[h:h + 1, :] for h in H]
    m_prev = [m_ref[:, h:h + 1] for h in H]

    d = [jnp.where(causal, b_c[h] - src_r[h], -jnp.inf) for h in H]
    a = [b_c[h] + m_prev[h] for h in H]
    m_row = [jnp.maximum(a[h], jnp.max(d[h], axis=1, keepdims=True)) for h in H]

    q = [head_cols(q_src, h) for h in H]
    ks = [head_cols(k_src, h) * jnp.asarray(ML_DH ** -0.5, BF16) for h in H]
    v = [head_cols(v_src, h) for h in H]
    qk = [lax.dot_general(q[h], ks[h], NT_DIMS, preferred_element_type=F32) for h in H]
    s = [qk[h] * jnp.exp(d[h] - m_row[h]) for h in H]
    w_inter = [jnp.exp(a[h] - m_row[h]) for h in H]

    c = [c_ref[h] for h in H]
    n = [n_ref[h:h + 1, :] for h in H]
    qc = [lax.dot_general(q[h], c[h].astype(BF16), NT_DIMS, preferred_element_type=F32) for h in H]
    sv = [jnp.dot(s[h].astype(BF16), v[h], preferred_element_type=F32) for h in H]
    num = [w_inter[h] * qc[h] + sv[h] for h in H]
    qn = [jnp.sum(q[h].astype(F32) * n[h], axis=1, keepdims=True) for h in H]
    den = [w_inter[h] * qn[h] + jnp.sum(s[h], axis=1, keepdims=True) for h in H]
    hh = [num[h] * (1.0 / jnp.maximum(jnp.abs(den[h]), jnp.exp(-m_row[h]))) for h in H]

    b_last = [b_c[h][L - 1:L, :] for h in H]
    g_c = [b_last[h] - b_c[h] + ig_c[h] for h in H]
    m_new = [jnp.maximum(b_last[h] + m_prev[h], jnp.max(g_c[h], axis=0, keepdims=True)) for h in H]
    w_old = [jnp.exp(b_last[h] + m_prev[h] - m_new[h]) for h in H]
    w_s = [jnp.exp(g_c[h] - m_new[h]) for h in H]
    wv = [(w_s[h] * v[h].astype(F32)).astype(BF16) for h in H]
    for h in H:
        c_ref[h] = w_old[h] * c[h] + lax.dot_general(wv[h], ks[h], TN_DIMS, preferred_element_type=F32)
        n_ref[h:h + 1, :] = w_old[h] * n[h] + jnp.sum(w_s[h] * ks[h].astype(F32), axis=0, keepdims=True)
        m_ref[:, h:h + 1] = m_new[h]

    hn = [hh[h] * lax.rsqrt(jnp.mean(hh[h] * hh[h], axis=1, keepdims=True) + RMS_EPS) * gh_ref[:, hs[h]] for h in H]
    for h in H:
        y_ref[:, hs[h]] = (hn[h] * _sigmoid(head_cols(o_src, h).astype(F32))).astype(BF16)


def _conv_init(ext_ref, s0_ref):
    ext_ref[...] = jnp.zeros_like(ext_ref)
    if s0_ref is not None:
        for c in range(N_SLABS):
            ext_ref[c, CONV_PAD - CONV_BUF:CONV_PAD, :] = s0_ref[:, c * LANES:(c + 1) * LANES]


def _conv_store(u, ext_ref, *, tt):
    for c in range(N_SLABS):
        ext_ref[c, CONV_PAD:CONV_PAD + tt, :] = u[:, c * LANES:(c + 1) * LANES]


def _conv_taps(c, w_ref, cb_ref, ext_ref, out_ref, *, tt):
    seg = (tt + CONV_PAD) // SUBLANES
    bias = cb_ref[c]
    for p0 in range(0, seg, CONV_POS_GROUP):
        p1 = min(p0 + CONV_POS_GROUP, seg)
        acc = [None] * (p1 - p0)
        for q in range(p0, p1 + CONV_W - 1):
            z = ext_ref[c, pl.ds(q, SUBLANES, stride=seg), :]
            for p in range(max(p0, q - CONV_W + 1), min(p1, q + 1)):
                term = z * w_ref[q - p, c]
                acc[p - p0] = term if acc[p - p0] is None else acc[p - p0] + term
        for p in range(p0, p1):
            out_ref[c, pl.ds(p + CONV_BUF, SUBLANES, stride=seg), :] = acc[p - p0] + bias


def _conv_finish(lng_ref, lnb_ref, y_ref, so_ref, ext_ref, out_ref, *, tt):
    lo = CONV_PAD - CONV_BUF
    for c in range(N_SLABS):
        so_ref[:, c * LANES:(c + 1) * LANES] = ext_ref[c, tt + lo:tt + CONV_PAD, :]
        ext_ref[c, 0:CONV_PAD, :] = ext_ref[c, tt:tt + CONV_PAD, :]

    x = jnp.concatenate([out_ref[c, CONV_PAD:CONV_PAD + tt, :] for c in range(N_SLABS)], axis=1)
    mu = jnp.mean(x, axis=-1, keepdims=True)
    xc = x - mu
    yn = xc * lax.rsqrt(jnp.mean(xc * xc, axis=-1, keepdims=True) + LN_EPS) * lng_ref[...] + lnb_ref[...]
    y_ref[...] = (yn * _sigmoid(yn)).astype(BF16)


def _conv_scratch(tt):
    slab = pltpu.VMEM((N_SLABS, tt + 2 * CONV_PAD, LANES), F32)
    return [slab, slab]


def _xattn_body(q_src, kb_ref, vb_ref, y_ref):
    q_ref, first = q_src
    for h in range(XA_HEADS):
        hs = slice(h * XA_DH, (h + 1) * XA_DH)
        q = q_ref[:, first + h * XA_DH:first + (h + 1) * XA_DH]
        s = lax.dot_general(q, kb_ref[:, hs], NT_DIMS, preferred_element_type=F32)
        s = s * (XA_DH ** -0.5)
        e = jnp.exp(s - jnp.max(s, axis=-1, keepdims=True))
        p = e * (1.0 / jnp.sum(e, axis=-1, keepdims=True))
        y_ref[:, hs] = jnp.dot(p.astype(BF16), vb_ref[:, hs], preferred_element_type=F32).astype(BF16)


def _branches_kernel(x_ref, gpre_ref, w_ref, wg_ref, wgt_ref, brow_ref, bcol_ref, gh_ref,
                     cw_ref, cb_ref, lng_ref, lnb_ref, mk_ref, mv_ref, wb_ref, wo_ref, gpost_ref,
                     xo_ref, c_ref, n_ref, m_ref, so_ref,
                     xn_scr, p_scr, y_scr, g_scr, ext_ref, out_ref, kb_ref, vb_ref, *, tt):
    @pl.when(pl.program_id(1) == 0)
    def _():
        c_ref[...] = jnp.zeros_like(c_ref)
        n_ref[...] = jnp.zeros_like(n_ref)
        m_ref[...] = jnp.zeros_like(m_ref)
        _conv_init(ext_ref, None)
        kb_ref[...] = mk_ref[...].astype(BF16)
        vb_ref[...] = mv_ref[...].astype(BF16)

    xn_scr[...] = _rms(x_ref[...], gpre_ref[...]).astype(BF16)
    proj = lambda c: jnp.dot(xn_scr[...], w_ref[c], preferred_element_type=F32)
    _conv_store(proj(COL_GA) * _sigmoid(proj(COL_GB)), ext_ref, tt=tt)

    for j, c in enumerate(FUSED_COLS):
        p_scr[j] = proj(c).astype(BF16)
    for c in range(N_SLABS):
        _conv_taps(c, cw_ref, cb_ref, ext_ref, out_ref, tt=tt)
    for j in range(N_BRANCH):
        g_scr[j] = proj(COL_GATE + j).astype(BF16)
    xn = xn_scr[...]
    gates_c = jnp.dot(xn, wg_ref[...], preferred_element_type=F32) + brow_ref[...]
    gates_r = lax.dot_general(wgt_ref[...], xn, NT_DIMS, preferred_element_type=F32) + bcol_ref[...]

    yml_ref, ycv_ref, yxa_ref = (y_scr.at[j] for j in range(N_BRANCH))
    _conv_finish(lng_ref, lnb_ref, ycv_ref, so_ref, ext_ref, out_ref, tt=tt)
    src = lambda c: (p_scr.at[FUSED_COLS.index(c)], 0)
    _mlstm_body(src(COL_Q), src(COL_K), src(COL_V), src(COL_O), gates_c, gates_r, gh_ref,
                yml_ref, c_ref, n_ref, m_ref, L=tt)
    _xattn_body(src(COL_XQ), kb_ref, vb_ref, yxa_ref)
    _merge_kernel(x_ref, yml_ref, ycv_ref, yxa_ref, g_scr.at[0], g_scr.at[1], g_scr.at[2],
                  wb_ref, wo_ref, gpost_ref, xo_ref)


def _branches(x2d, wts, l, mk, mv, *, batch, seq, tt):
    nt = seq // tt
    m = batch * seq
    rows = lambda b, t: (b * nt + t, 0)
    tile = lambda width: pl.BlockSpec((tt, width), rows)
    kv_spec = pl.BlockSpec((None, MEM_LEN, D_MODEL), lambda b, t: (b, 0, 0))
    outs = pl.pallas_call(
        functools.partial(_branches_kernel, tt=tt),
        grid=(batch, nt),
        in_specs=[
            tile(D_MODEL), _layer_vec(l),
            _resident((None, N_CHUNKS, D_MODEL, D_MODEL), lambda b, t: (l, 0, 0, 0)),
            _resident((None, D_MODEL, GATE_PAD), lambda b, t: (l, 0, 0)),
            _resident((None, GATE_ROWS, D_MODEL), lambda b, t: (l, 0, 0)),
            _layer_vec(l, GATE_PAD),
            pl.BlockSpec((None, GATE_ROWS, 1), lambda b, t: (l, 0, 0)),
            _layer_vec(l),
            pl.BlockSpec((None, CONV_W, N_SLABS, SUBLANES, LANES), lambda b, t: (l, 0, 0, 0, 0)),
            pl.BlockSpec((None, N_SLABS, SUBLANES, LANES), lambda b, t: (l, 0, 0, 0)),
            _layer_vec(l), _layer_vec(l),
            kv_spec, kv_spec,
            _resident((None, N_BRANCH, D_MODEL, D_MODEL), lambda b, t: (l, 0, 0, 0)),
            _resident((None, D_MODEL, D_MODEL), lambda b, t: (l, 0, 0)),
            _layer_vec(l),
        ],
        out_specs=[
            tile(D_MODEL),
            pl.BlockSpec((None, ML_HEADS, ML_DH, ML_DH), lambda b, t: (b, 0, 0, 0)),
            pl.BlockSpec((None, ML_HEADS, ML_DH), lambda b, t: (b, 0, 0)),
            pl.BlockSpec((None, 1, ML_HEADS), lambda b, t: (b, 0, 0)),
            pl.BlockSpec((None, CONV_BUF, D_MODEL), lambda b, t: (b, 0, 0)),
        ],
        out_shape=[
            jax.ShapeDtypeStruct((m, D_MODEL), F32),
            jax.ShapeDtypeStruct((batch, ML_HEADS, ML_DH, ML_DH), F32),
            jax.ShapeDtypeStruct((batch, ML_HEADS, ML_DH), F32),
            jax.ShapeDtypeStruct((batch, 1, ML_HEADS), F32),
            jax.ShapeDtypeStruct((batch, CONV_BUF, D_MODEL), F32),
        ],
        scratch_shapes=[pltpu.VMEM((tt, D_MODEL), BF16), pltpu.VMEM((len(FUSED_COLS), tt, D_MODEL), BF16),
                        pltpu.VMEM((N_BRANCH, tt, D_MODEL), BF16), pltpu.VMEM((N_BRANCH, tt, D_MODEL), BF16)]
                       + _conv_scratch(tt)
                       + [pltpu.VMEM((MEM_LEN, D_MODEL), BF16), pltpu.VMEM((MEM_LEN, D_MODEL), BF16)],
        compiler_params=_params(("parallel", "arbitrary")),
        name="branches",
    )(x2d, wts["g_mix_pre"], wts["w_main"], wts["w_gate"], wts["w_gate_t"], wts["bias_row"], wts["bias_col"],
      wts["g_mhead"], wts["conv_w"], wts["conv_b"], wts["ln_g"], wts["ln_b"], mk, mv,
      wts["w_branch"], wts["w_out"], wts["g_mix_post"])
    x_mid, c, n, mm, conv_out = outs
    return x_mid, (c, n, mm.reshape(batch, ML_HEADS)), conv_out


def _inproj_kernel(x_ref, g_ref, w_ref, wg_ref, wgt_ref, p_ref, go_ref, gto_ref):
    xn = _rms(x_ref[...], g_ref[...]).astype(BF16)
    for c in range(N_MAIN // D_MODEL):
        p_ref[:, _col(c)] = jnp.dot(xn, w_ref[c], preferred_element_type=F32).astype(BF16)
    go_ref[...] = jnp.dot(xn, wg_ref[...], preferred_element_type=F32)
    gto_ref[...] = lax.dot_general(wgt_ref[...], xn, NT_DIMS, preferred_element_type=F32)


def _inproj(x2d, wts, l, *, batch, seq):
    m = x2d.shape[0]
    p, go, gto = pl.pallas_call(
        _inproj_kernel,
        grid=(1,),
        in_specs=[
            pl.BlockSpec((m, D_MODEL), lambda i: (0, 0)),
            _layer_vec(l),
            _resident((None, N_CHUNKS, D_MODEL, D_MODEL), lambda i: (l, 0, 0, 0)),
            _resident((None, D_MODEL, GATE_PAD), lambda i: (l, 0, 0)),
            _resident((None, GATE_ROWS, D_MODEL), lambda i: (l, 0, 0)),
        ],
        out_specs=[
            pl.BlockSpec((m, N_MAIN), lambda i: (0, 0)),
            pl.BlockSpec((m, GATE_PAD), lambda i: (0, 0)),
            pl.BlockSpec((GATE_ROWS, m), lambda i: (0, 0)),
        ],
        out_shape=[
            jax.ShapeDtypeStruct((m, N_MAIN), BF16),
            jax.ShapeDtypeStruct((m, GATE_PAD), F32),
            jax.ShapeDtypeStruct((GATE_ROWS, m), F32),
        ],
        compiler_params=_params(("arbitrary",)),
        name="inproj",
    )(x2d, wts["g_mix_pre"], wts["w_main"], wts["w_gate"], wts["w_gate_t"])
    return p, go, jnp.transpose(gto.reshape(GATE_ROWS, batch, seq), (1, 0, 2))


def _mlstm_kernel(q_ref, k_ref, v_ref, o_ref, g_ref, gt_ref, brow_ref, bcol_ref, gh_ref, c0_ref, n0_ref, m0_ref,
                  y_ref, c_ref, n_ref, m_ref, *, L):
    @pl.when(pl.program_id(1) == 0)
    def _():
        c_ref[...] = c0_ref[...]
        n_ref[...] = n0_ref[...]
        m_ref[...] = m0_ref[...]

    _mlstm_body((q_ref, 0), (k_ref, 0), (v_ref, 0), (o_ref, 0),
                g_ref[...] + brow_ref[...], gt_ref[...] + bcol_ref[...], gh_ref,
                y_ref, c_ref, n_ref, m_ref, L=L)


def _mlstm(p, go, gto, wts, l, state, *, batch, seq, L):
    nb = seq // L
    m = batch * seq
    rows = lambda b, t: b * nb + t
    c0, n0, m0 = state
    y, c, n, mm = pl.pallas_call(
        functools.partial(_mlstm_kernel, L=L),
        grid=(batch, nb),
        in_specs=[
            pl.BlockSpec((L, D_MODEL), lambda b, t: (rows(b, t), COL_Q)),
            pl.BlockSpec((L, D_MODEL), lambda b, t: (rows(b, t), COL_K)),
            pl.BlockSpec((L, D_MODEL), lambda b, t: (rows(b, t), COL_V)),
            pl.BlockSpec((L, D_MODEL), lambda b, t: (rows(b, t), COL_O)),
            pl.BlockSpec((L, GATE_PAD), lambda b, t: (rows(b, t), 0)),
            pl.BlockSpec((None, GATE_ROWS, L), lambda b, t: (b, 0, t)),
            _layer_vec(l, GATE_PAD),
            pl.BlockSpec((None, GATE_ROWS, 1), lambda b, t: (l, 0, 0)),
            _layer_vec(l),
            pl.BlockSpec((None, None, ML_HEADS, ML_DH, ML_DH), lambda b, t: (l, b, 0, 0, 0)),
            pl.BlockSpec((None, None, ML_HEADS, ML_DH), lambda b, t: (l, b, 0, 0)),
            pl.BlockSpec((None, None, 1, ML_HEADS), lambda b, t: (l, b, 0, 0)),
        ],
        out_specs=[
            pl.BlockSpec((L, D_MODEL), lambda b, t: (rows(b, t), 0)),
            pl.BlockSpec((None, ML_HEADS, ML_DH, ML_DH), lambda b, t: (b, 0, 0, 0)),
            pl.BlockSpec((None, ML_HEADS, ML_DH), lambda b, t: (b, 0, 0)),
            pl.BlockSpec((None, 1, ML_HEADS), lambda b, t: (b, 0, 0)),
        ],
        out_shape=[
            jax.ShapeDtypeStruct((m, D_MODEL), BF16),
            jax.ShapeDtypeStruct((batch, ML_HEADS, ML_DH, ML_DH), F32),
            jax.ShapeDtypeStruct((batch, ML_HEADS, ML_DH), F32),
            jax.ShapeDtypeStruct((batch, 1, ML_HEADS), F32),
        ],
        compiler_params=_params(("parallel", "arbitrary")),
        name="mlstm",
    )(p, p, p, p, go, gto, wts["bias_row"], wts["bias_col"], wts["g_mhead"], c0, n0, m0)
    return y, (c, n, mm.reshape(batch, ML_HEADS))


def _conv_kernel(ga_ref, gb_ref, w_ref, cb_ref, lng_ref, lnb_ref, s0_ref, y_ref, so_ref, ext_ref, out_ref, *, tt):
    @pl.when(pl.program_id(1) == 0)
    def _():
        _conv_init(ext_ref, s0_ref)

    u = ga_ref[...].astype(F32) * _sigmoid(gb_ref[...].astype(F32))
    _conv_store(u, ext_ref, tt=tt)
    for c in range(N_SLABS):
        _conv_taps(c, w_ref, cb_ref, ext_ref, out_ref, tt=tt)
    _conv_finish(lng_ref, lnb_ref, y_ref, so_ref, ext_ref, out_ref, tt=tt)


def _conv(p, wts, l, state, *, batch, seq, tt):
    nt = seq // tt
    m = batch * seq
    rows = lambda b, t: b * nt + t
    return pl.pallas_call(
        functools.partial(_conv_kernel, tt=tt),
        grid=(batch, nt),
        in_specs=[
            pl.BlockSpec((tt, D_MODEL), lambda b, t: (rows(b, t), COL_GA)),
            pl.BlockSpec((tt, D_MODEL), lambda b, t: (rows(b, t), COL_GB)),
            pl.BlockSpec((None, CONV_W, N_SLABS, SUBLANES, LANES), lambda b, t: (l, 0, 0, 0, 0)),
            pl.BlockSpec((None, N_SLABS, SUBLANES, LANES), lambda b, t: (l, 0, 0, 0)),
            _layer_vec(l), _layer_vec(l),
            pl.BlockSpec((None, None, CONV_BUF, D_MODEL), lambda b, t: (l, b, 0, 0)),
        ],
        out_specs=[pl.BlockSpec((tt, D_MODEL), lambda b, t: (rows(b, t), 0)),
                   pl.BlockSpec((None, CONV_BUF, D_MODEL), lambda b, t: (b, 0, 0))],
        out_shape=[
            jax.ShapeDtypeStruct((m, D_MODEL), BF16),
            jax.ShapeDtypeStruct((batch, CONV_BUF, D_MODEL), F32),
        ],
        scratch_shapes=_conv_scratch(tt),
        compiler_params=_params(("parallel", "arbitrary")),
        name="conv",
    )(p, p, wts["conv_w"], wts["conv_b"], wts["ln_g"], wts["ln_b"], state)


def _xattn_kernel(q_ref, k_ref, v_ref, y_ref, kb_ref, vb_ref):
    @pl.when(pl.program_id(1) == 0)
    def _():
        kb_ref[...] = k_ref[...].astype(BF16)
        vb_ref[...] = v_ref[...].astype(BF16)

    _xattn_body((q_ref, 0), kb_ref, vb_ref, y_ref)


def _xattn(p, mk, mv, l, *, batch, seq, tt):
    nt = seq // tt
    m = batch * seq
    kv_spec = pl.BlockSpec((None, None, MEM_LEN, D_MODEL), lambda b, t: (l, b, 0, 0))
    return pl.pallas_call(
        _xattn_kernel,
        grid=(batch, nt),
        in_specs=[pl.BlockSpec((tt, D_MODEL), lambda b, t: (b * nt + t, COL_XQ)), kv_spec, kv_spec],
        out_specs=pl.BlockSpec((tt, D_MODEL), lambda b, t: (b * nt + t, 0)),
        out_shape=jax.ShapeDtypeStruct((m, D_MODEL), BF16),
        scratch_shapes=[pltpu.VMEM((MEM_LEN, D_MODEL), BF16), pltpu.VMEM((MEM_LEN, D_MODEL), BF16)],
        compiler_params=_params(("parallel", "arbitrary")),
        name="xattn",
    )(p, mk, mv)


def _merge_kernel(x_ref, yml_ref, ycv_ref, yxa_ref, g0_ref, g1_ref, g2_ref, wb_ref, wo_ref, gp_ref, o_ref):
    mixed = None
    for i, (y_ref, gate_ref) in enumerate(((yml_ref, g0_ref), (ycv_ref, g1_ref), (yxa_ref, g2_ref))):
        term = _sigmoid(gate_ref[...].astype(F32)) * jnp.dot(y_ref[...], wb_ref[i], preferred_element_type=F32)
        mixed = term if mixed is None else mixed + term
    z = jnp.dot(mixed.astype(BF16), wo_ref[...], preferred_element_type=F32)
    o_ref[...] = x_ref[...] + _rms(z, gp_ref[...])


def _merge(x2d, y_ml, y_cv, y_xa, gates, gate_col, wts, l, *, tm):
    m = x2d.shape[0]
    tile = lambda c: pl.BlockSpec((tm, D_MODEL), lambda i: (i, c))
    return pl.pallas_call(
        _merge_kernel,
        grid=(m // tm,),
        in_specs=[
            tile(0), tile(0), tile(0), tile(0),
            tile(gate_col), tile(gate_col + 1), tile(gate_col + 2),
            _resident((None, N_BRANCH, D_MODEL, D_MODEL), lambda i: (l, 0, 0, 0)),
            _resident((None, D_MODEL, D_MODEL), lambda i: (l, 0, 0)),
            _layer_vec(l),
        ],
        out_specs=tile(0),
        out_shape=jax.ShapeDtypeStruct((m, D_MODEL), F32),
        compiler_params=_params(("parallel",)),
        name="merge",
    )(x2d, y_ml, y_cv, y_xa, gates, gates, gates, wts["w_branch"], wts["w_out"], wts["g_mix_post"])


FFN_CHUNKS = ((0, 768), (768, 768), (1536, 768), (2304, 512))


def _ffn_kernel(x_ref, gpre_ref, wa_ref, wb_ref, wo_ref, gpost_ref, o_ref, h_ref):
    x = x_ref[...]
    xn = _rms(x, gpre_ref[...]).astype(BF16)
    for start, size in FFN_CHUNKS:
        sl = slice(start, start + size)
        fa = jnp.dot(xn, wa_ref[:, sl], preferred_element_type=F32)
        fb = jnp.dot(xn, wb_ref[:, sl], preferred_element_type=F32)
        h_ref[:, sl] = (fa * _sigmoid(fa) * fb).astype(BF16)
    z = jnp.dot(h_ref[...], wo_ref[...], preferred_element_type=F32)
    o_ref[...] = x + _rms(z, gpost_ref[...])


def _ffn(x2d, wts, l, *, tm):
    m = x2d.shape[0]
    return pl.pallas_call(
        _ffn_kernel,
        grid=(m // tm,),
        in_specs=[
            pl.BlockSpec((tm, D_MODEL), lambda i: (i, 0)),
            _layer_vec(l),
            _resident((None, D_MODEL, D_FF), lambda i: (l, 0, 0)),
            _resident((None, D_MODEL, D_FF), lambda i: (l, 0, 1)),
            _resident((None, D_FF, D_MODEL), lambda i: (l, 0, 0)),
            _layer_vec(l),
        ],
        out_specs=pl.BlockSpec((tm, D_MODEL), lambda i: (i, 0)),
        out_shape=jax.ShapeDtypeStruct((m, D_MODEL), F32),
        scratch_shapes=[pltpu.VMEM((tm, D_FF), BF16)],
        compiler_params=_params(("parallel",)),
        name="ffn",
    )(x2d, wts["g_ffn_pre"], wts["w_ffn_in"], wts["w_ffn_in"], wts["w_ffn_out"], wts["g_ffn_post"])


def _memkv_kernel(x_ref, g_ref, w_ref, k_ref, v_ref):
    xn = _rms(x_ref[...], g_ref[...]).astype(BF16)
    k_ref[...] = jnp.dot(xn, w_ref[:, :D_MODEL], preferred_element_type=F32)
    v_ref[...] = jnp.dot(xn, w_ref[:, D_MODEL:], preferred_element_type=F32)


def _memkv(mem2d, wts, l, *, tm):
    m = mem2d.shape[0]
    out = jax.ShapeDtypeStruct((m, D_MODEL), F32)
    tile = pl.BlockSpec((tm, D_MODEL), lambda i: (i, 0))
    return pl.pallas_call(
        _memkv_kernel,
        grid=(m // tm,),
        in_specs=[tile, _layer_vec(l), _resident((None, D_MODEL, 2 * D_MODEL), lambda i: (l, 0, 0))],
        out_specs=[tile, tile],
        out_shape=[out, out],
        compiler_params=_params(("parallel",)),
        name="memkv",
    )(mem2d, wts["g_mem"], wts["w_mem_kv"])


def _layer_first_chunks(x2d, wts, l, mk, mv, *, batch, seq):
    tiles = _tiles(batch, seq)
    x2d, ml_state, conv_out = _branches(x2d, wts, l, mk, mv, batch=batch, seq=seq, tt=tiles["L"])
    x2d = _ffn(x2d, wts, l, tm=tiles["tm"])
    return x2d, ml_state, conv_out


def _layer_later_chunk(x2d, wts, l, ml_state, conv_state, mk, mv, *, batch, seq):
    tiles = _tiles(batch, seq)
    p, go, gto = _inproj(x2d, wts, l, batch=batch, seq=seq)
    y_ml, ml_state = _mlstm(p, go, gto, wts, l, ml_state, batch=batch, seq=seq, L=tiles["L"])
    y_cv, conv_out = _conv(p, wts, l, conv_state, batch=batch, seq=seq, tt=tiles["L"])
    y_xa = _xattn(p, mk, mv, l, batch=batch, seq=seq, tt=tiles["tt_xa"])
    x2d = _merge(x2d, y_ml, y_cv, y_xa, p, COL_GATE, wts, l, tm=tiles["tm"])
    x2d = _ffn(x2d, wts, l, tm=tiles["tm"])
    return x2d, ml_state, conv_out


def kernel(x_prompt, x_sample, mem_prompt, state_mlstm_c, state_mlstm_n, state_mlstm_m, state_conv,
           cache_mem_k, cache_mem_v, norm_mix_pre, norm_mix_post, norm_ffn_pre, norm_ffn_post, w_in, b_gate,
           mlstm_head_norm, conv_w, conv_b, conv_ln_g, conv_ln_b, mem_norm, w_mem_kv, w_branch, w_out,
           w_ffn_in, w_ffn_out):
    depth = w_in.shape[0]
    bp, sp, _ = x_prompt.shape
    bs, ss, _ = x_sample.shape
    gate_lo = 4 * D_MODEL
    gate_hi = gate_lo + N_GATES
    chunk_starts = [c * D_MODEL + (N_GATES if c * D_MODEL >= gate_lo else 0) for c in range(N_CHUNKS)]

    vec = lambda v: v.reshape(depth, 1, -1).astype(F32)
    w_in_bf16 = lax.optimization_barrier(w_in.astype(BF16))
    w_gate_cols = w_in_bf16[:, :, gate_lo:gate_hi]
    wts = {
        "w_main": jnp.stack([w_in_bf16[:, :, a:a + D_MODEL] for a in chunk_starts], axis=1),
        "w_gate": jnp.pad(w_gate_cols, ((0, 0), (0, 0), (0, GATE_PAD - N_GATES))).astype(BF16),
        "w_gate_t": jnp.pad(jnp.swapaxes(w_gate_cols, 1, 2), ((0, 0), (0, GATE_ROWS - N_GATES), (0, 0))).astype(BF16),
        "bias_row": jnp.pad(b_gate, ((0, 0), (0, GATE_PAD - N_GATES))).reshape(depth, 1, GATE_PAD).astype(F32),
        "bias_col": jnp.pad(b_gate, ((0, 0), (0, GATE_ROWS - N_GATES))).reshape(depth, GATE_ROWS, 1).astype(F32),
        "g_mix_pre": vec(norm_mix_pre), "g_mix_post": vec(norm_mix_post),
        "g_ffn_pre": vec(norm_ffn_pre), "g_ffn_post": vec(norm_ffn_post),
        "g_mhead": vec(mlstm_head_norm), "g_mem": vec(mem_norm),
        "conv_w": jnp.broadcast_to(conv_w.astype(F32).reshape(depth, CONV_W, N_SLABS, 1, LANES),
                                   (depth, CONV_W, N_SLABS, SUBLANES, LANES)),
        "conv_b": jnp.broadcast_to(conv_b.astype(F32).reshape(depth, N_SLABS, 1, LANES),
                                   (depth, N_SLABS, SUBLANES, LANES)),
        "ln_g": vec(conv_ln_g), "ln_b": vec(conv_ln_b),
        "w_branch": w_branch.astype(BF16), "w_out": w_out.astype(BF16),
        "w_ffn_in": w_ffn_in.astype(BF16), "w_ffn_out": w_ffn_out.astype(BF16),
        "w_mem_kv": w_mem_kv.astype(BF16),
    }
    s_state = (state_mlstm_c.astype(F32), state_mlstm_n.astype(F32),
               state_mlstm_m.astype(F32).reshape(depth, bs, 1, ML_HEADS))
    s_conv = state_conv.astype(F32)
    cache_k = cache_mem_k.reshape(depth, bs, MEM_LEN, D_MODEL)
    cache_v = cache_mem_v.reshape(depth, bs, MEM_LEN, D_MODEL)

    yp = x_prompt.reshape(bp * sp, D_MODEL)
    ys = x_sample.reshape(bs * ss, D_MODEL)
    mem2d = mem_prompt.reshape(bp * MEM_LEN, D_MODEL)

    pc, pn, pm, pconv, pmk, pmv = [], [], [], [], [], []
    sc, sn, sm, sconv = [], [], [], []
    for l in range(depth):
        mk_p, mv_p = _memkv(mem2d, wts, l, tm=256)
        yp, (c1, n1, m1), buf1 = _layer_first_chunks(
            yp, wts, l, mk_p.reshape(bp, MEM_LEN, D_MODEL), mv_p.reshape(bp, MEM_LEN, D_MODEL), batch=bp, seq=sp)
        ys, (c2, n2, m2), buf2 = _layer_later_chunk(ys, wts, l, s_state, s_conv, cache_k, cache_v, batch=bs, seq=ss)
        pc.append(c1); pn.append(n1); pm.append(m1); pconv.append(buf1)
        pmk.append(mk_p.reshape(bp, MEM_LEN, XA_HEADS, XA_DH)); pmv.append(mv_p.reshape(bp, MEM_LEN, XA_HEADS, XA_DH))
        sc.append(c2); sn.append(n2); sm.append(m2); sconv.append(buf2)
    return (yp.reshape(bp, sp, D_MODEL), ys.reshape(bs, ss, D_MODEL),
            jnp.stack(pc), jnp.stack(pn), jnp.stack(pm), jnp.stack(pconv), jnp.stack(pmk), jnp.stack(pmv),
            jnp.stack(sc), jnp.stack(sn), jnp.stack(sm), jnp.stack(sconv))
```

```python
import functools

import jax
import jax.numpy as jnp
from jax import lax
from jax.experimental import pallas as pl
from jax.experimental.pallas import tpu as pltpu

F32 = jnp.float32
BF16 = jnp.bfloat16

D_MODEL = 1024
ML_HEADS = 4
ML_DH = 256
XA_HEADS = 4
XA_DH = 256
MEM_LEN = 256
CONV_W = 31
CONV_BUF = CONV_W - 1
D_FF = 2816
RMS_EPS = 1e-6
LN_EPS = 1e-5

N_MAIN = 10 * D_MODEL
COL_Q, COL_K, COL_V, COL_O, COL_GA, COL_GB, COL_XQ, COL_GATE = 0, 1, 2, 3, 4, 5, 6, 7
N_BRANCH = 3
N_GATES = 2 * ML_HEADS
GATE_PAD = 128
GATE_ROWS = 16
FUSED_COLS = (COL_Q, COL_K, COL_V, COL_O, COL_XQ)

V7X_VMEM_LIMIT_BYTES = 56 * 1024 * 1024
SUBLANES = 8
LANES = 128
N_SLABS = D_MODEL // LANES
N_CHUNKS = N_MAIN // D_MODEL
N_LO = COL_O + 1


def _w_chunk(w_lo_ref, w_hi_ref, c):
    return w_lo_ref[:, _col(c)] if c < N_LO else w_hi_ref[:, _col(c - N_LO)]


def _w_specs(l):
    return [_resident((None, D_MODEL, N_LO * D_MODEL), lambda *_: (l, 0, 0)),
            _resident((None, D_MODEL, (N_CHUNKS - N_LO) * D_MODEL), lambda *_: (l, 0, 0))]
CONV_PAD = 32
CONV_POS_GROUP = 12

NT_DIMS = (((1,), (1,)), ((), ()))
TN_DIMS = (((0,), (0,)), ((), ()))

_sigmoid = jax.nn.sigmoid


def _tiles(batch, seq):
    if seq >= 512:
        return dict(tm=512, L=256, tt_xa=512)
    return dict(tm=batch * seq, L=seq, tt_xa=seq)


def _resident(block_shape, index_map):
    return pl.BlockSpec(block_shape, index_map, pipeline_mode=pl.Buffered(1))


def _layer_vec(l, width=D_MODEL):
    return pl.BlockSpec((None, 1, width), lambda *_: (l, 0, 0))


def _params(semantics):
    return pltpu.CompilerParams(dimension_semantics=semantics, vmem_limit_bytes=V7X_VMEM_LIMIT_BYTES)


def _col(c):
    return slice(c * D_MODEL, (c + 1) * D_MODEL)


def _rms(x, g):
    return x * lax.rsqrt(jnp.mean(x * x, axis=-1, keepdims=True) + RMS_EPS) * g


def _mlstm_body(q_src, k_src, v_src, o_src, gates_c, gates_r, gh_ref, y_ref, c_ref, n_ref, m_ref, *, L):
    row = lax.broadcasted_iota(jnp.int32, (L, L), 0)
    col = lax.broadcasted_iota(jnp.int32, (L, L), 1)
    causal = col <= row
    tril = causal.astype(F32)
    triu = (row <= col).astype(F32)
    cum_c = jnp.dot(tril, jax.nn.log_sigmoid(gates_c), precision=lax.Precision.HIGHEST,
                    preferred_element_type=F32)
    cum_r = jnp.dot(jax.nn.log_sigmoid(gates_r), triu, precision=lax.Precision.HIGHEST,
                    preferred_element_type=F32)

    def head_cols(src, h):
        ref, first = src
        return ref[:, first + h * ML_DH:first + (h + 1) * ML_DH]

    H = range(ML_HEADS)
    hs = [slice(h * ML_DH, (h + 1) * ML_DH) for h in H]
    ig_c = [gates_c[:, h:h + 1] for h in H]
    b_c = [cum_c[:, ML_HEADS + h:ML_HEADS + h + 1] for h in H]
    src_r = [cum_r[ML_HEADS + h:ML_HEADS + h + 1, :] - gates_r[h:h + 1, :] for h in H]
    m_prev = [m_ref[:, h:h + 1] for h in H]

    d = [jnp.where(causal, b_c[h] - src_r[h], -jnp.inf) for h in H]
    a = [b_c[h] + m_prev[h] for h in H]
    m_row = [jnp.maximum(a[h], jnp.max(d[h], axis=1, keepdims=True)) for h in H]

    q = [head_cols(q_src, h) for h in H]
    ks = [head_cols(k_src, h) * jnp.asarray(ML_DH ** -0.5, BF16) for h in H]
    v = [head_cols(v_src, h) for h in H]
    qk = [lax.dot_general(q[h], ks[h], NT_DIMS, preferred_element_type=F32) for h in H]
    s = [qk[h] * jnp.exp(d[h] - m_row[h]) for h in H]
    w_inter = [jnp.exp(a[h] - m_row[h]) for h in H]

    c = [c_ref[h] for h in H]
    n = [n_ref[h:h + 1, :] for h in H]
    qc = [lax.dot_general(q[h], c[h].astype(BF16), NT_DIMS, preferred_element_type=F32) for h in H]
    sv = [jnp.dot(s[h].astype(BF16), v[h], preferred_element_type=F32) for h in H]
    num = [w_inter[h] * qc[h] + sv[h] for h in H]
    qn = [jnp.sum(q[h].astype(F32) * n[h], axis=1, keepdims=True) for h in H]
    den = [w_inter[h] * qn[h] + jnp.sum(s[h], axis=1, keepdims=True) for h in H]
    hh = [num[h] * (1.0 / jnp.maximum(jnp.abs(den[h]), jnp.exp(-m_row[h]))) for h in H]

    b_last = [b_c[h][L - 1:L, :] for h in H]
    g_c = [b_last[h] - b_c[h] + ig_c[h] for h in H]
    m_new = [jnp.maximum(b_last[h] + m_prev[h], jnp.max(g_c[h], axis=0, keepdims=True)) for h in H]
    w_old = [jnp.exp(b_last[h] + m_prev[h] - m_new[h]) for h in H]
    w_s = [jnp.exp(g_c[h] - m_new[h]) for h in H]
    wv = [(w_s[h] * v[h].astype(F32)).astype(BF16) for h in H]
    for h in H:
        c_ref[h] = w_old[h] * c[h] + lax.dot_general(wv[h], ks[h], TN_DIMS, preferred_element_type=F32)
        n_ref[h:h + 1, :] = w_old[h] * n[h] + jnp.sum(w_s[h] * ks[h].astype(F32), axis=0, keepdims=True)
        m_ref[:, h:h + 1] = m_new[h]

    hn = [hh[h] * lax.rsqrt(jnp.mean(hh[h] * hh[h], axis=1, keepdims=True) + RMS_EPS) * gh_ref[:, hs[h]] for h in H]
    for h in H:
        y_ref[:, hs[h]] = (hn[h] * _sigmoid(head_cols(o_src, h).astype(F32))).astype(BF16)


def _conv_init(ext_ref, s0_ref):
    ext_ref[...] = jnp.zeros_like(ext_ref)
    if s0_ref is not None:
        for c in range(N_SLABS):
            ext_ref[c, CONV_PAD - CONV_BUF:CONV_PAD, :] = s0_ref[:, c * LANES:(c + 1) * LANES]


def _conv_store(u, ext_ref, *, tt):
    for c in range(N_SLABS):
        ext_ref[c, CONV_PAD:CONV_PAD + tt, :] = u[:, c * LANES:(c + 1) * LANES]


def _conv_taps(c, w_ref, cb_ref, ext_ref, out_ref, *, tt):
    seg = (tt + CONV_PAD) // SUBLANES
    bias = cb_ref[c]
    for p0 in range(0, seg, CONV_POS_GROUP):
        p1 = min(p0 + CONV_POS_GROUP, seg)
        acc = [None] * (p1 - p0)
        for q in range(p0, p1 + CONV_W - 1):
            z = ext_ref[c, pl.ds(q, SUBLANES, stride=seg), :]
            for p in range(max(p0, q - CONV_W + 1), min(p1, q + 1)):
                term = z * w_ref[q - p, c]
                acc[p - p0] = term if acc[p - p0] is None else acc[p - p0] + term
        for p in range(p0, p1):
            out_ref[c, pl.ds(p + CONV_BUF, SUBLANES, stride=seg), :] = acc[p - p0] + bias


def _conv_finish(lng_ref, lnb_ref, y_ref, so_ref, ext_ref, out_ref, *, tt):
    lo = CONV_PAD - CONV_BUF
    for c in range(N_SLABS):
        so_ref[:, c * LANES:(c + 1) * LANES] = ext_ref[c, tt + lo:tt + CONV_PAD, :]
        ext_ref[c, 0:CONV_PAD, :] = ext_ref[c, tt:tt + CONV_PAD, :]

    x = jnp.concatenate([out_ref[c, CONV_PAD:CONV_PAD + tt, :] for c in range(N_SLABS)], axis=1)
    mu = jnp.mean(x, axis=-1, keepdims=True)
    xc = x - mu
    yn = xc * lax.rsqrt(jnp.mean(xc * xc, axis=-1, keepdims=True) + LN_EPS) * lng_ref[...] + lnb_ref[...]
    y_ref[...] = (yn * _sigmoid(yn)).astype(BF16)


def _conv_scratch(tt):
    slab = pltpu.VMEM((N_SLABS, tt + 2 * CONV_PAD, LANES), F32)
    return [slab, slab]


def _xattn_body(q_src, kb_ref, vb_ref, y_ref):
    q_ref, first = q_src
    for h in range(XA_HEADS):
        hs = slice(h * XA_DH, (h + 1) * XA_DH)
        q = q_ref[:, first + h * XA_DH:first + (h + 1) * XA_DH]
        s = lax.dot_general(q, kb_ref[:, hs], NT_DIMS, preferred_element_type=F32)
        s = s * (XA_DH ** -0.5)
        e = jnp.exp(s - jnp.max(s, axis=-1, keepdims=True))
        p = e * (1.0 / jnp.sum(e, axis=-1, keepdims=True))
        y_ref[:, hs] = jnp.dot(p.astype(BF16), vb_ref[:, hs], preferred_element_type=F32).astype(BF16)


def _branches_kernel(x_ref, gpre_ref, wlo_ref, whi_ref, wg_ref, wgt_ref, brow_ref, bcol_ref, gh_ref,
                     cw_ref, cb_ref, lng_ref, lnb_ref, mk_ref, mv_ref, wb_ref, wo_ref, gpost_ref,
                     xo_ref, c_ref, n_ref, m_ref, so_ref,
                     xn_scr, p_scr, y_scr, g_scr, ext_ref, out_ref, kb_ref, vb_ref, *, tt):
    @pl.when(pl.program_id(1) == 0)
    def _():
        c_ref[...] = jnp.zeros_like(c_ref)
        n_ref[...] = jnp.zeros_like(n_ref)
        m_ref[...] = jnp.zeros_like(m_ref)
        _conv_init(ext_ref, None)
        kb_ref[...] = mk_ref[...].astype(BF16)
        vb_ref[...] = mv_ref[...].astype(BF16)

    xn_scr[...] = _rms(x_ref[...], gpre_ref[...]).astype(BF16)
    proj = lambda c: jnp.dot(xn_scr[...], _w_chunk(wlo_ref, whi_ref, c), preferred_element_type=F32)
    _conv_store(proj(COL_GA) * _sigmoid(proj(COL_GB)), ext_ref, tt=tt)

    for j, c in enumerate(FUSED_COLS):
        p_scr[j] = proj(c).astype(BF16)
    for c in range(N_SLABS):
        _conv_taps(c, cw_ref, cb_ref, ext_ref, out_ref, tt=tt)
    for j in range(N_BRANCH):
        g_scr[j] = proj(COL_GATE + j).astype(BF16)
    xn = xn_scr[...]
    gates_c = jnp.dot(xn, wg_ref[...], preferred_element_type=F32) + brow_ref[...]
    gates_r = lax.dot_general(wgt_ref[...], xn, NT_DIMS, preferred_element_type=F32) + bcol_ref[...]

    yml_ref, ycv_ref, yxa_ref = (y_scr.at[j] for j in range(N_BRANCH))
    _conv_finish(lng_ref, lnb_ref, ycv_ref, so_ref, ext_ref, out_ref, tt=tt)
    src = lambda c: (p_scr.at[FUSED_COLS.index(c)], 0)
    _mlstm_body(src(COL_Q), src(COL_K), src(COL_V), src(COL_O), gates_c, gates_r, gh_ref,
                yml_ref, c_ref, n_ref, m_ref, L=tt)
    _xattn_body(src(COL_XQ), kb_ref, vb_ref, yxa_ref)
    _merge_kernel(x_ref, yml_ref, ycv_ref, yxa_ref, g_scr.at[0], g_scr.at[1], g_scr.at[2],
                  wb_ref, wo_ref, gpost_ref, xo_ref)


def _branches(x2d, wts, l, mk, mv, *, batch, seq, tt):
    nt = seq // tt
    m = batch * seq
    rows = lambda b, t: (b * nt + t, 0)
    tile = lambda width: pl.BlockSpec((tt, width), rows)
    kv_spec = pl.BlockSpec((None, MEM_LEN, D_MODEL), lambda b, t: (b, 0, 0))
    outs = pl.pallas_call(
        functools.partial(_branches_kernel, tt=tt),
        grid=(batch, nt),
        in_specs=[
            tile(D_MODEL), _layer_vec(l), *_w_specs(l),
            _resident((None, D_MODEL, GATE_PAD), lambda b, t: (l, 0, 0)),
            _resident((None, GATE_ROWS, D_MODEL), lambda b, t: (l, 0, 0)),
            _layer_vec(l, GATE_PAD),
            pl.BlockSpec((None, GATE_ROWS, 1), lambda b, t: (l, 0, 0)),
            _layer_vec(l),
            pl.BlockSpec((None, CONV_W, N_SLABS, SUBLANES, LANES), lambda b, t: (l, 0, 0, 0, 0)),
            pl.BlockSpec((None, N_SLABS, SUBLANES, LANES), lambda b, t: (l, 0, 0, 0)),
            _layer_vec(l), _layer_vec(l),
            kv_spec, kv_spec,
            _resident((None, N_BRANCH, D_MODEL, D_MODEL), lambda b, t: (l, 0, 0, 0)),
            _resident((None, D_MODEL, D_MODEL), lambda b, t: (l, 0, 0)),
            _layer_vec(l),
        ],
        out_specs=[
            tile(D_MODEL),
            pl.BlockSpec((None, ML_HEADS, ML_DH, ML_DH), lambda b, t: (b, 0, 0, 0)),
            pl.BlockSpec((None, ML_HEADS, ML_DH), lambda b, t: (b, 0, 0)),
            pl.BlockSpec((None, 1, ML_HEADS), lambda b, t: (b, 0, 0)),
            pl.BlockSpec((None, CONV_BUF, D_MODEL), lambda b, t: (b, 0, 0)),
        ],
        out_shape=[
            jax.ShapeDtypeStruct((m, D_MODEL), F32),
            jax.ShapeDtypeStruct((batch, ML_HEADS, ML_DH, ML_DH), F32),
            jax.ShapeDtypeStruct((batch, ML_HEADS, ML_DH), F32),
            jax.ShapeDtypeStruct((batch, 1, ML_HEADS), F32),
            jax.ShapeDtypeStruct((batch, CONV_BUF, D_MODEL), F32),
        ],
        scratch_shapes=[pltpu.VMEM((tt, D_MODEL), BF16), pltpu.VMEM((len(FUSED_COLS), tt, D_MODEL), BF16),
                        pltpu.VMEM((N_BRANCH, tt, D_MODEL), BF16), pltpu.VMEM((N_BRANCH, tt, D_MODEL), BF16)]
                       + _conv_scratch(tt)
                       + [pltpu.VMEM((MEM_LEN, D_MODEL), BF16), pltpu.VMEM((MEM_LEN, D_MODEL), BF16)],
        compiler_params=_params(("parallel", "arbitrary")),
        name="branches",
    )(x2d, wts["g_mix_pre"], wts["w_lo"], wts["w_hi"], wts["w_gate"], wts["w_gate_t"], wts["bias_row"], wts["bias_col"],
      wts["g_mhead"], wts["conv_w"], wts["conv_b"], wts["ln_g"], wts["ln_b"], mk, mv,
      wts["w_branch"], wts["w_out"], wts["g_mix_post"])
    x_mid, c, n, mm, conv_out = outs
    return x_mid, (c, n, mm.reshape(batch, ML_HEADS)), conv_out


def _inproj_kernel(x_ref, g_ref, wlo_ref, whi_ref, wg_ref, wgt_ref, p_ref, go_ref, gto_ref):
    xn = _rms(x_ref[...], g_ref[...]).astype(BF16)
    for c in range(N_MAIN // D_MODEL):
        p_ref[:, _col(c)] = jnp.dot(xn, _w_chunk(wlo_ref, whi_ref, c), preferred_element_type=F32).astype(BF16)
    go_ref[...] = jnp.dot(xn, wg_ref[...], preferred_element_type=F32)
    gto_ref[...] = lax.dot_general(wgt_ref[...], xn, NT_DIMS, preferred_element_type=F32)


def _inproj(x2d, wts, l, *, batch, seq):
    m = x2d.shape[0]
    p, go, gto = pl.pallas_call(
        _inproj_kernel,
        grid=(1,),
        in_specs=[
            pl.BlockSpec((m, D_MODEL), lambda i: (0, 0)),
            _layer_vec(l), *_w_specs(l),
            _resident((None, D_MODEL, GATE_PAD), lambda i: (l, 0, 0)),
            _resident((None, GATE_ROWS, D_MODEL), lambda i: (l, 0, 0)),
        ],
        out_specs=[
            pl.BlockSpec((m, N_MAIN), lambda i: (0, 0)),
            pl.BlockSpec((m, GATE_PAD), lambda i: (0, 0)),
            pl.BlockSpec((GATE_ROWS, m), lambda i: (0, 0)),
        ],
        out_shape=[
            jax.ShapeDtypeStruct((m, N_MAIN), BF16),
            jax.ShapeDtypeStruct((m, GATE_PAD), F32),
            jax.ShapeDtypeStruct((GATE_ROWS, m), F32),
        ],
        compiler_params=_params(("arbitrary",)),
        name="inproj",
    )(x2d, wts["g_mix_pre"], wts["w_lo"], wts["w_hi"], wts["w_gate"], wts["w_gate_t"])
    return p, go, jnp.transpose(gto.reshape(GATE_ROWS, batch, seq), (1, 0, 2))


def _mlstm_kernel(q_ref, k_ref, v_ref, o_ref, g_ref, gt_ref, brow_ref, bcol_ref, gh_ref, c0_ref, n0_ref, m0_ref,
                  y_ref, c_ref, n_ref, m_ref, *, L):
    @pl.when(pl.program_id(1) == 0)
    def _():
        c_ref[...] = c0_ref[...]
        n_ref[...] = n0_ref[...]
        m_ref[...] = m0_ref[...]

    _mlstm_body((q_ref, 0), (k_ref, 0), (v_ref, 0), (o_ref, 0),
                g_ref[...] + brow_ref[...], gt_ref[...] + bcol_ref[...], gh_ref,
                y_ref, c_ref, n_ref, m_ref, L=L)


def _mlstm(p, go, gto, wts, l, state, *, batch, seq, L):
    nb = seq // L
    m = batch * seq
    rows = lambda b, t: b * nb + t
    c0, n0, m0 = state
    y, c, n, mm = pl.pallas_call(
        functools.partial(_mlstm_kernel, L=L),
        grid=(batch, nb),
        in_specs=[
            pl.BlockSpec((L, D_MODEL), lambda b, t: (rows(b, t), COL_Q)),
            pl.BlockSpec((L, D_MODEL), lambda b, t: (rows(b, t), COL_K)),
            pl.BlockSpec((L, D_MODEL), lambda b, t: (rows(b, t), COL_V)),
            pl.BlockSpec((L, D_MODEL), lambda b, t: (rows(b, t), COL_O)),
            pl.BlockSpec((L, GATE_PAD), lambda b, t: (rows(b, t), 0)),
            pl.BlockSpec((None, GATE_ROWS, L), lambda b, t: (b, 0, t)),
            _layer_vec(l, GATE_PAD),
            pl.BlockSpec((None, GATE_ROWS, 1), lambda b, t: (l, 0, 0)),
            _layer_vec(l),
            pl.BlockSpec((None, None, ML_HEADS, ML_DH, ML_DH), lambda b, t: (l, b, 0, 0, 0)),
            pl.BlockSpec((None, None, ML_HEADS, ML_DH), lambda b, t: (l, b, 0, 0)),
            pl.BlockSpec((None, None, 1, ML_HEADS), lambda b, t: (l, b, 0, 0)),
        ],
        out_specs=[
            pl.BlockSpec((L, D_MODEL), lambda b, t: (rows(b, t), 0)),
            pl.BlockSpec((None, ML_HEADS, ML_DH, ML_DH), lambda b, t: (b, 0, 0, 0)),
            pl.BlockSpec((None, ML_HEADS, ML_DH), lambda b, t: (b, 0, 0)),
            pl.BlockSpec((None, 1, ML_HEADS), lambda b, t: (b, 0, 0)),
        ],
        out_shape=[
            jax.ShapeDtypeStruct((m, D_MODEL), BF16),
            jax.ShapeDtypeStruct((batch, ML_HEADS, ML_DH, ML_DH), F32),
            jax.ShapeDtypeStruct((batch, ML_HEADS, ML_DH), F32),
            jax.ShapeDtypeStruct((batch, 1, ML_HEADS), F32),
        ],
        compiler_params=_params(("parallel", "arbitrary")),
        name="mlstm",
    )(p, p, p, p, go, gto, wts["bias_row"], wts["bias_col"], wts["g_mhead"], c0, n0, m0)
    return y, (c, n, mm.reshape(batch, ML_HEADS))


def _conv_kernel(ga_ref, gb_ref, w_ref, cb_ref, lng_ref, lnb_ref, s0_ref, y_ref, so_ref, ext_ref, out_ref, *, tt):
    @pl.when(pl.program_id(1) == 0)
    def _():
        _conv_init(ext_ref, s0_ref)

    u = ga_ref[...].astype(F32) * _sigmoid(gb_ref[...].astype(F32))
    _conv_store(u, ext_ref, tt=tt)
    for c in range(N_SLABS):
        _conv_taps(c, w_ref, cb_ref, ext_ref, out_ref, tt=tt)
    _conv_finish(lng_ref, lnb_ref, y_ref, so_ref, ext_ref, out_ref, tt=tt)


def _conv(p, wts, l, state, *, batch, seq, tt):
    nt = seq // tt
    m = batch * seq
    rows = lambda b, t: b * nt + t
    return pl.pallas_call(
        functools.partial(_conv_kernel, tt=tt),
        grid=(batch, nt),
        in_specs=[
            pl.BlockSpec((tt, D_MODEL), lambda b, t: (rows(b, t), COL_GA)),
            pl.BlockSpec((tt, D_MODEL), lambda b, t: (rows(b, t), COL_GB)),
            pl.BlockSpec((None, CONV_W, N_SLABS, SUBLANES, LANES), lambda b, t: (l, 0, 0, 0, 0)),
            pl.BlockSpec((None, N_SLABS, SUBLANES, LANES), lambda b, t: (l, 0, 0, 0)),
            _layer_vec(l), _layer_vec(l),
            pl.BlockSpec((None, None, CONV_BUF, D_MODEL), lambda b, t: (l, b, 0, 0)),
        ],
        out_specs=[pl.BlockSpec((tt, D_MODEL), lambda b, t: (rows(b, t), 0)),
                   pl.BlockSpec((None, CONV_BUF, D_MODEL), lambda b, t: (b, 0, 0))],
        out_shape=[
            jax.ShapeDtypeStruct((m, D_MODEL), BF16),
            jax.ShapeDtypeStruct((batch, CONV_BUF, D_MODEL), F32),
        ],
        scratch_shapes=_conv_scratch(tt),
        compiler_params=_params(("parallel", "arbitrary")),
        name="conv",
    )(p, p, wts["conv_w"], wts["conv_b"], wts["ln_g"], wts["ln_b"], state)


def _xattn_kernel(q_ref, k_ref, v_ref, y_ref, kb_ref, vb_ref):
    @pl.when(pl.program_id(1) == 0)
    def _():
        kb_ref[...] = k_ref[...].astype(BF16)
        vb_ref[...] = v_ref[...].astype(BF16)

    _xattn_body((q_ref, 0), kb_ref, vb_ref, y_ref)


def _xattn(p, mk, mv, l, *, batch, seq, tt):
    nt = seq // tt
    m = batch * seq
    kv_spec = pl.BlockSpec((None, None, MEM_LEN, D_MODEL), lambda b, t: (l, b, 0, 0))
    return pl.pallas_call(
        _xattn_kernel,
        grid=(batch, nt),
        in_specs=[pl.BlockSpec((tt, D_MODEL), lambda b, t: (b * nt + t, COL_XQ)), kv_spec, kv_spec],
        out_specs=pl.BlockSpec((tt, D_MODEL), lambda b, t: (b * nt + t, 0)),
        out_shape=jax.ShapeDtypeStruct((m, D_MODEL), BF16),
        scratch_shapes=[pltpu.VMEM((MEM_LEN, D_MODEL), BF16), pltpu.VMEM((MEM_LEN, D_MODEL), BF16)],
        compiler_params=_params(("parallel", "arbitrary")),
        name="xattn",
    )(p, mk, mv)


def _merge_kernel(x_ref, yml_ref, ycv_ref, yxa_ref, g0_ref, g1_ref, g2_ref, wb_ref, wo_ref, gp_ref, o_ref):
    mixed = None
    for i, (y_ref, gate_ref) in enumerate(((yml_ref, g0_ref), (ycv_ref, g1_ref), (yxa_ref, g2_ref))):
        term = _sigmoid(gate_ref[...].astype(F32)) * jnp.dot(y_ref[...], wb_ref[i], preferred_element_type=F32)
        mixed = term if mixed is None else mixed + term
    z = jnp.dot(mixed.astype(BF16), wo_ref[...], preferred_element_type=F32)
    o_ref[...] = x_ref[...] + _rms(z, gp_ref[...])


def _merge(x2d, y_ml, y_cv, y_xa, gates, gate_col, wts, l, *, tm):
    m = x2d.shape[0]
    tile = lambda c: pl.BlockSpec((tm, D_MODEL), lambda i: (i, c))
    return pl.pallas_call(
        _merge_kernel,
        grid=(m // tm,),
        in_specs=[
            tile(0), tile(0), tile(0), tile(0),
            tile(gate_col), tile(gate_col + 1), tile(gate_col + 2),
            _resident((None, N_BRANCH, D_MODEL, D_MODEL), lambda i: (l, 0, 0, 0)),
            _resident((None, D_MODEL, D_MODEL), lambda i: (l, 0, 0)),
            _layer_vec(l),
        ],
        out_specs=tile(0),
        out_shape=jax.ShapeDtypeStruct((m, D_MODEL), F32),
        compiler_params=_params(("parallel",)),
        name="merge",
    )(x2d, y_ml, y_cv, y_xa, gates, gates, gates, wts["w_branch"], wts["w_out"], wts["g_mix_post"])


FFN_CHUNKS = ((0, 768), (768, 768), (1536, 768), (2304, 512))


def _ffn_kernel(x_ref, gpre_ref, wa_ref, wb_ref, wo_ref, gpost_ref, o_ref, h_ref):
    x = x_ref[...]
    xn = _rms(x, gpre_ref[...]).astype(BF16)
    for start, size in FFN_CHUNKS:
        sl = slice(start, start + size)
        fa = jnp.dot(xn, wa_ref[:, sl], preferred_element_type=F32)
        fb = jnp.dot(xn, wb_ref[:, sl], preferred_element_type=F32)
        h_ref[:, sl] = (fa * _sigmoid(fa) * fb).astype(BF16)
    z = jnp.dot(h_ref[...], wo_ref[...], preferred_element_type=F32)
    o_ref[...] = x + _rms(z, gpost_ref[...])


def _ffn(x2d, wts, l, *, tm):
    m = x2d.shape[0]
    return pl.pallas_call(
        _ffn_kernel,
        grid=(m // tm,),
        in_specs=[
            pl.BlockSpec((tm, D_MODEL), lambda i: (i, 0)),
            _layer_vec(l),
            _resident((None, D_MODEL, D_FF), lambda i: (l, 0, 0)),
            _resident((None, D_MODEL, D_FF), lambda i: (l, 0, 1)),
            _resident((None, D_FF, D_MODEL), lambda i: (l, 0, 0)),
            _layer_vec(l),
        ],
        out_specs=pl.BlockSpec((tm, D_MODEL), lambda i: (i, 0)),
        out_shape=jax.ShapeDtypeStruct((m, D_MODEL), F32),
        scratch_shapes=[pltpu.VMEM((tm, D_FF), BF16)],
        compiler_params=_params(("parallel",)),
        name="ffn",
    )(x2d, wts["g_ffn_pre"], wts["w_ffn_in"], wts["w_ffn_in"], wts["w_ffn_out"], wts["g_ffn_post"])


def _memkv_kernel(x_ref, g_ref, w_ref, k_ref, v_ref):
    xn = _rms(x_ref[...], g_ref[...]).astype(BF16)
    k_ref[...] = jnp.dot(xn, w_ref[:, :D_MODEL], preferred_element_type=F32)
    v_ref[...] = jnp.dot(xn, w_ref[:, D_MODEL:], preferred_element_type=F32)


def _memkv(mem2d, wts, l, *, tm):
    m = mem2d.shape[0]
    out = jax.ShapeDtypeStruct((m, D_MODEL), F32)
    tile = pl.BlockSpec((tm, D_MODEL), lambda i: (i, 0))
    return pl.pallas_call(
        _memkv_kernel,
        grid=(m // tm,),
        in_specs=[tile, _layer_vec(l), _resident((None, D_MODEL, 2 * D_MODEL), lambda i: (l, 0, 0))],
        out_specs=[tile, tile],
        out_shape=[out, out],
        compiler_params=_params(("parallel",)),
        name="memkv",
    )(mem2d, wts["g_mem"], wts["w_mem_kv"])


def _layer_first_chunks(x2d, wts, l, mk, mv, *, batch, seq):
    tiles = _tiles(batch, seq)
    x2d, ml_state, conv_out = _branches(x2d, wts, l, mk, mv, batch=batch, seq=seq, tt=tiles["L"])
    x2d = _ffn(x2d, wts, l, tm=tiles["tm"])
    return x2d, ml_state, conv_out


def _layer_later_chunk(x2d, wts, l, ml_state, conv_state, mk, mv, *, batch, seq):
    tiles = _tiles(batch, seq)
    p, go, gto = _inproj(x2d, wts, l, batch=batch, seq=seq)
    y_ml, ml_state = _mlstm(p, go, gto, wts, l, ml_state, batch=batch, seq=seq, L=tiles["L"])
    y_cv, conv_out = _conv(p, wts, l, conv_state, batch=batch, seq=seq, tt=tiles["L"])
    y_xa = _xattn(p, mk, mv, l, batch=batch, seq=seq, tt=tiles["tt_xa"])
    x2d = _merge(x2d, y_ml, y_cv, y_xa, p, COL_GATE, wts, l, tm=tiles["tm"])
    x2d = _ffn(x2d, wts, l, tm=tiles["tm"])
    return x2d, ml_state, conv_out


def kernel(x_prompt, x_sample, mem_prompt, state_mlstm_c, state_mlstm_n, state_mlstm_m, state_conv,
           cache_mem_k, cache_mem_v, norm_mix_pre, norm_mix_post, norm_ffn_pre, norm_ffn_post, w_in, b_gate,
           mlstm_head_norm, conv_w, conv_b, conv_ln_g, conv_ln_b, mem_norm, w_mem_kv, w_branch, w_out,
           w_ffn_in, w_ffn_out):
    depth = w_in.shape[0]
    bp, sp, _ = x_prompt.shape
    bs, ss, _ = x_sample.shape
    gate_lo = 4 * D_MODEL
    gate_hi = gate_lo + N_GATES

    vec = lambda v: v.reshape(depth, 1, -1).astype(F32)
    w_gate_cols = w_in[:, :, gate_lo:gate_hi]
    wts = {
        "w_lo": w_in[:, :, :gate_lo].astype(BF16), "w_hi": w_in[:, :, gate_hi:].astype(BF16),
        "w_gate": jnp.pad(w_gate_cols, ((0, 0), (0, 0), (0, GATE_PAD - N_GATES))).astype(BF16),
        "w_gate_t": jnp.pad(jnp.swapaxes(w_gate_cols, 1, 2), ((0, 0), (0, GATE_ROWS - N_GATES), (0, 0))).astype(BF16),
        "bias_row": jnp.pad(b_gate, ((0, 0), (0, GATE_PAD - N_GATES))).reshape(depth, 1, GATE_PAD).astype(F32),
        "bias_col": jnp.pad(b_gate, ((0, 0), (0, GATE_ROWS - N_GATES))).reshape(depth, GATE_ROWS, 1).astype(F32),
        "g_mix_pre": vec(norm_mix_pre), "g_mix_post": vec(norm_mix_post),
        "g_ffn_pre": vec(norm_ffn_pre), "g_ffn_post": vec(norm_ffn_post),
        "g_mhead": vec(mlstm_head_norm), "g_mem": vec(mem_norm),
        "conv_w": jnp.broadcast_to(conv_w.astype(F32).reshape(depth, CONV_W, N_SLABS, 1, LANES),
                                   (depth, CONV_W, N_SLABS, SUBLANES, LANES)),
        "conv_b": jnp.broadcast_to(conv_b.astype(F32).reshape(depth, N_SLABS, 1, LANES),
                                   (depth, N_SLABS, SUBLANES, LANES)),
        "ln_g": vec(conv_ln_g), "ln_b": vec(conv_ln_b),
        "w_branch": w_branch.astype(BF16), "w_out": w_out.astype(BF16),
        "w_ffn_in": w_ffn_in.astype(BF16), "w_ffn_out": w_ffn_out.astype(BF16),
        "w_mem_kv": w_mem_kv.astype(BF16),
    }
    s_state = (state_mlstm_c.astype(F32), state_mlstm_n.astype(F32),
               state_mlstm_m.astype(F32).reshape(depth, bs, 1, ML_HEADS))
    s_conv = state_conv.astype(F32)
    cache_k = cache_mem_k.reshape(depth, bs, MEM_LEN, D_MODEL)
    cache_v = cache_mem_v.reshape(depth, bs, MEM_LEN, D_MODEL)

    yp = x_prompt.reshape(bp * sp, D_MODEL)
    ys = x_sample.reshape(bs * ss, D_MODEL)
    mem2d = mem_prompt.reshape(bp * MEM_LEN, D_MODEL)

    pc, pn, pm, pconv, pmk, pmv = [], [], [], [], [], []
    sc, sn, sm, sconv = [], [], [], []
    for l in range(depth):
        mk_p, mv_p = _memkv(mem2d, wts, l, tm=256)
        yp, (c1, n1, m1), buf1 = _layer_first_chunks(
            yp, wts, l, mk_p.reshape(bp, MEM_LEN, D_MODEL), mv_p.reshape(bp, MEM_LEN, D_MODEL), batch=bp, seq=sp)
        ys, (c2, n2, m2), buf2 = _layer_later_chunk(ys, wts, l, s_state, s_conv, cache_k, cache_v, batch=bs, seq=ss)
        pc.append(c1); pn.append(n1); pm.append(m1); pconv.append(buf1)
        pmk.append(mk_p.reshape(bp, MEM_LEN, XA_HEADS, XA_DH)); pmv.append(mv_p.reshape(bp, MEM_LEN, XA_HEADS, XA_DH))
        sc.append(c2); sn.append(n2); sm.append(m2); sconv.append(buf2)
    return (yp.reshape(bp, sp, D_MODEL), ys.reshape(bs, ss, D_MODEL),
            jnp.stack(pc), jnp.stack(pn), jnp.stack(pm), jnp.stack(pconv), jnp.stack(pmk), jnp.stack(pmv),
            jnp.stack(sc), jnp.stack(sn), jnp.stack(sm), jnp.stack(sconv))
```

```python
import functools

import jax
import jax.numpy as jnp
from jax import lax
from jax.experimental import pallas as pl
from jax.experimental.pallas import tpu as pltpu

F32 = jnp.float32
BF16 = jnp.bfloat16

D_MODEL = 1024
ML_HEADS = 4
ML_DH = 256
XA_HEADS = 4
XA_DH = 256
MEM_LEN = 256
CONV_W = 31
CONV_BUF = CONV_W - 1
D_FF = 2816
RMS_EPS = 1e-6
LN_EPS = 1e-5

N_MAIN = 10 * D_MODEL
COL_Q, COL_K, COL_V, COL_O, COL_GA, COL_GB, COL_XQ, COL_GATE = 0, 1, 2, 3, 4, 5, 6, 7
N_BRANCH = 3
N_GATES = 2 * ML_HEADS
GATE_PAD = 128
GATE_ROWS = 16
FUSED_COLS = (COL_Q, COL_K, COL_V, COL_O, COL_XQ)

V7X_VMEM_LIMIT_BYTES = 56 * 1024 * 1024
SUBLANES = 8
LANES = 128
N_SLABS = D_MODEL // LANES
N_CHUNKS = N_MAIN // D_MODEL
N_LO = COL_O + 1


def _w_chunk(w_lo_ref, w_hi_ref, c):
    return w_lo_ref[:, _col(c)] if c < N_LO else w_hi_ref[:, _col(c - N_LO)]


def _w_specs(l):
    return [_resident((None, D_MODEL, N_LO * D_MODEL), lambda *_: (l, 0, 0)),
            _resident((None, D_MODEL, (N_CHUNKS - N_LO) * D_MODEL), lambda *_: (l, 0, 0))]
CONV_PAD = 32
CONV_POS_GROUP = 12

NT_DIMS = (((1,), (1,)), ((), ()))
TN_DIMS = (((0,), (0,)), ((), ()))

_sigmoid = jax.nn.sigmoid


def _tiles(batch, seq):
    if seq >= 512:
        return dict(tm=1024, L=256, tt_xa=512)
    return dict(tm=batch * seq, L=seq, tt_xa=seq)


def _resident(block_shape, index_map):
    return pl.BlockSpec(block_shape, index_map, pipeline_mode=pl.Buffered(1))


def _layer_vec(l, width=D_MODEL):
    return pl.BlockSpec((None, 1, width), lambda *_: (l, 0, 0))


def _params(semantics):
    return pltpu.CompilerParams(dimension_semantics=semantics, vmem_limit_bytes=V7X_VMEM_LIMIT_BYTES)


def _col(c):
    return slice(c * D_MODEL, (c + 1) * D_MODEL)


def _rms(x, g):
    return x * lax.rsqrt(jnp.mean(x * x, axis=-1, keepdims=True) + RMS_EPS) * g


def _mlstm_body(q_src, k_src, v_src, o_src, gates_c, gates_r, gh_ref, y_ref, c_ref, n_ref, m_ref, *, L):
    row = lax.broadcasted_iota(jnp.int32, (L, L), 0)
    col = lax.broadcasted_iota(jnp.int32, (L, L), 1)
    causal = col <= row
    tril = causal.astype(F32)
    triu = (row <= col).astype(F32)
    cum_c = jnp.dot(tril, jax.nn.log_sigmoid(gates_c), precision=lax.Precision.HIGHEST,
                    preferred_element_type=F32)
    cum_r = jnp.dot(jax.nn.log_sigmoid(gates_r), triu, precision=lax.Precision.HIGHEST,
                    preferred_element_type=F32)

    def head_cols(src, h):
        ref, first = src
        return ref[:, first + h * ML_DH:first + (h + 1) * ML_DH]

    H = range(ML_HEADS)
    hs = [slice(h * ML_DH, (h + 1) * ML_DH) for h in H]
    ig_c = [gates_c[:, h:h + 1] for h in H]
    b_c = [cum_c[:, ML_HEADS + h:ML_HEADS + h + 1] for h in H]
    src_r = [cum_r[ML_HEADS + h:ML_HEADS + h + 1, :] - gates_r[h:h + 1, :] for h in H]
    m_prev = [m_ref[:, h:h + 1] for h in H]

    d = [jnp.where(causal, b_c[h] - src_r[h], -jnp.inf) for h in H]
    a = [b_c[h] + m_prev[h] for h in H]
    m_row = [jnp.maximum(a[h], jnp.max(d[h], axis=1, keepdims=True)) for h in H]

    q = [head_cols(q_src, h) for h in H]
    ks = [head_cols(k_src, h) * jnp.asarray(ML_DH ** -0.5, BF16) for h in H]
    v = [head_cols(v_src, h) for h in H]
    qk = [lax.dot_general(q[h], ks[h], NT_DIMS, preferred_element_type=F32) for h in H]
    s = [qk[h] * jnp.exp(d[h] - m_row[h]) for h in H]
    w_inter = [jnp.exp(a[h] - m_row[h]) for h in H]

    c = [c_ref[h] for h in H]
    n = [n_ref[h:h + 1, :] for h in H]
    qc = [lax.dot_general(q[h], c[h].astype(BF16), NT_DIMS, preferred_element_type=F32) for h in H]
    sv = [jnp.dot(s[h].astype(BF16), v[h], preferred_element_type=F32) for h in H]
    num = [w_inter[h] * qc[h] + sv[h] for h in H]
    qn = [jnp.sum(q[h].astype(F32) * n[h], axis=1, keepdims=True) for h in H]
    den = [w_inter[h] * qn[h] + jnp.sum(s[h], axis=1, keepdims=True) for h in H]
    hh = [num[h] * (1.0 / jnp.maximum(jnp.abs(den[h]), jnp.exp(-m_row[h]))) for h in H]

    b_last = [b_c[h][L - 1:L, :] for h in H]
    g_c = [b_last[h] - b_c[h] + ig_c[h] for h in H]
    m_new = [jnp.maximum(b_last[h] + m_prev[h], jnp.max(g_c[h], axis=0, keepdims=True)) for h in H]
    w_old = [jnp.exp(b_last[h] + m_prev[h] - m_new[h]) for h in H]
    w_s = [jnp.exp(g_c[h] - m_new[h]) for h in H]
    wv = [(w_s[h] * v[h].astype(F32)).astype(BF16) for h in H]
    for h in H:
        c_ref[h] = w_old[h] * c[h] + lax.dot_general(wv[h], ks[h], TN_DIMS, preferred_element_type=F32)
        n_ref[h:h + 1, :] = w_old[h] * n[h] + jnp.sum(w_s[h] * ks[h].astype(F32), axis=0, keepdims=True)
        m_ref[:, h:h + 1] = m_new[h]

    hn = [hh[h] * lax.rsqrt(jnp.mean(hh[h] * hh[h], axis=1, keepdims=True) + RMS_EPS) * gh_ref[:, hs[h]] for h in H]
    for h in H:
        y_ref[:, hs[h]] = (hn[h] * _sigmoid(head_cols(o_src, h).astype(F32))).astype(BF16)


def _conv_init(ext_ref, s0_ref):
    ext_ref[...] = jnp.zeros_like(ext_ref)
    if s0_ref is not None:
        for c in range(N_SLABS):
            ext_ref[c, CONV_PAD - CONV_BUF:CONV_PAD, :] = s0_ref[:, c * LANES:(c + 1) * LANES]


def _conv_store(u, ext_ref, *, tt):
    for c in range(N_SLABS):
        ext_ref[c, CONV_PAD:CONV_PAD + tt, :] = u[:, c * LANES:(c + 1) * LANES]


def _conv_taps(c, w_ref, cb_ref, ext_ref, out_ref, *, tt):
    seg = (tt + CONV_PAD) // SUBLANES
    bias = cb_ref[c]
    for p0 in range(0, seg, CONV_POS_GROUP):
        p1 = min(p0 + CONV_POS_GROUP, seg)
        acc = [None] * (p1 - p0)
        for q in range(p0, p1 + CONV_W - 1):
            z = ext_ref[c, pl.ds(q, SUBLANES, stride=seg), :]
            for p in range(max(p0, q - CONV_W + 1), min(p1, q + 1)):
                term = z * w_ref[q - p, c]
                acc[p - p0] = term if acc[p - p0] is None else acc[p - p0] + term
        for p in range(p0, p1):
            out_ref[c, pl.ds(p + CONV_BUF, SUBLANES, stride=seg), :] = acc[p - p0] + bias


def _conv_finish(lng_ref, lnb_ref, y_ref, so_ref, ext_ref, out_ref, *, tt):
    lo = CONV_PAD - CONV_BUF
    for c in range(N_SLABS):
        so_ref[:, c * LANES:(c + 1) * LANES] = ext_ref[c, tt + lo:tt + CONV_PAD, :]
        ext_ref[c, 0:CONV_PAD, :] = ext_ref[c, tt:tt + CONV_PAD, :]

    x = jnp.concatenate([out_ref[c, CONV_PAD:CONV_PAD + tt, :] for c in range(N_SLABS)], axis=1)
    mu = jnp.mean(x, axis=-1, keepdims=True)
    xc = x - mu
    yn = xc * lax.rsqrt(jnp.mean(xc * xc, axis=-1, keepdims=True) + LN_EPS) * lng_ref[...] + lnb_ref[...]
    y_ref[...] = (yn * _sigmoid(yn)).astype(BF16)


def _conv_scratch(tt):
    slab = pltpu.VMEM((N_SLABS, tt + 2 * CONV_PAD, LANES), F32)
    return [slab, slab]


def _xattn_body(q_src, kb_ref, vb_ref, y_ref):
    q_ref, first = q_src
    for h in range(XA_HEADS):
        hs = slice(h * XA_DH, (h + 1) * XA_DH)
        q = q_ref[:, first + h * XA_DH:first + (h + 1) * XA_DH]
        s = lax.dot_general(q, kb_ref[:, hs], NT_DIMS, preferred_element_type=F32)
        s = s * (XA_DH ** -0.5)
        e = jnp.exp(s - jnp.max(s, axis=-1, keepdims=True))
        p = e * (1.0 / jnp.sum(e, axis=-1, keepdims=True))
        y_ref[:, hs] = jnp.dot(p.astype(BF16), vb_ref[:, hs], preferred_element_type=F32).astype(BF16)


def _branches_kernel(x_ref, gpre_ref, wlo_ref, whi_ref, wg_ref, wgt_ref, brow_ref, bcol_ref, gh_ref,
                     cw_ref, cb_ref, lng_ref, lnb_ref, mk_ref, mv_ref, wb_ref, wo_ref, gpost_ref,
                     xo_ref, c_ref, n_ref, m_ref, so_ref,
                     xn_scr, p_scr, y_scr, g_scr, ext_ref, out_ref, kb_ref, vb_ref, *, tt):
    @pl.when(pl.program_id(1) == 0)
    def _():
        c_ref[...] = jnp.zeros_like(c_ref)
        n_ref[...] = jnp.zeros_like(n_ref)
        m_ref[...] = jnp.zeros_like(m_ref)
        _conv_init(ext_ref, None)
        kb_ref[...] = mk_ref[...].astype(BF16)
        vb_ref[...] = mv_ref[...].astype(BF16)

    xn_scr[...] = _rms(x_ref[...], gpre_ref[...]).astype(BF16)
    proj = lambda c: jnp.dot(xn_scr[...], _w_chunk(wlo_ref, whi_ref, c), preferred_element_type=F32)
    _conv_store(proj(COL_GA) * _sigmoid(proj(COL_GB)), ext_ref, tt=tt)

    for j, c in enumerate(FUSED_COLS):
        p_scr[j] = proj(c).astype(BF16)
    for c in range(N_SLABS):
        _conv_taps(c, cw_ref, cb_ref, ext_ref, out_ref, tt=tt)
    for j in range(N_BRANCH):
        g_scr[j] = proj(COL_GATE + j).astype(BF16)
    xn = xn_scr[...]
    gates_c = jnp.dot(xn, wg_ref[...], preferred_element_type=F32) + brow_ref[...]
    gates_r = lax.dot_general(wgt_ref[...], xn, NT_DIMS, preferred_element_type=F32) + bcol_ref[...]

    yml_ref, ycv_ref, yxa_ref = (y_scr.at[j] for j in range(N_BRANCH))
    _conv_finish(lng_ref, lnb_ref, ycv_ref, so_ref, ext_ref, out_ref, tt=tt)
    src = lambda c: (p_scr.at[FUSED_COLS.index(c)], 0)
    _mlstm_body(src(COL_Q), src(COL_K), src(COL_V), src(COL_O), gates_c, gates_r, gh_ref,
                yml_ref, c_ref, n_ref, m_ref, L=tt)
    _xattn_body(src(COL_XQ), kb_ref, vb_ref, yxa_ref)
    _merge_kernel(x_ref, yml_ref, ycv_ref, yxa_ref, g_scr.at[0], g_scr.at[1], g_scr.at[2],
                  wb_ref, wo_ref, gpost_ref, xo_ref)


def _branches(x2d, wts, l, mk, mv, *, batch, seq, tt):
    nt = seq // tt
    m = batch * seq
    rows = lambda b, t: (b * nt + t, 0)
    tile = lambda width: pl.BlockSpec((tt, width), rows)
    kv_spec = pl.BlockSpec((None, MEM_LEN, D_MODEL), lambda b, t: (b, 0, 0))
    outs = pl.pallas_call(
        functools.partial(_branches_kernel, tt=tt),
        grid=(batch, nt),
        in_specs=[
            tile(D_MODEL), _layer_vec(l), *_w_specs(l),
            _resident((None, D_MODEL, GATE_PAD), lambda b, t: (l, 0, 0)),
            _resident((None, GATE_ROWS, D_MODEL), lambda b, t: (l, 0, 0)),
            _layer_vec(l, GATE_PAD),
            pl.BlockSpec((None, GATE_ROWS, 1), lambda b, t: (l, 0, 0)),
            _layer_vec(l),
            pl.BlockSpec((None, CONV_W, N_SLABS, SUBLANES, LANES), lambda b, t: (l, 0, 0, 0, 0)),
            pl.BlockSpec((None, N_SLABS, SUBLANES, LANES), lambda b, t: (l, 0, 0, 0)),
            _layer_vec(l), _layer_vec(l),
            kv_spec, kv_spec,
            _resident((None, N_BRANCH, D_MODEL, D_MODEL), lambda b, t: (l, 0, 0, 0)),
            _resident((None, D_MODEL, D_MODEL), lambda b, t: (l, 0, 0)),
            _layer_vec(l),
        ],
        out_specs=[
            tile(D_MODEL),
            pl.BlockSpec((None, ML_HEADS, ML_DH, ML_DH), lambda b, t: (b, 0, 0, 0)),
            pl.BlockSpec((None, ML_HEADS, ML_DH), lambda b, t: (b, 0, 0)),
            pl.BlockSpec((None, 1, ML_HEADS), lambda b, t: (b, 0, 0)),
            pl.BlockSpec((None, CONV_BUF, D_MODEL), lambda b, t: (b, 0, 0)),
        ],
        out_shape=[
            jax.ShapeDtypeStruct((m, D_MODEL), F32),
            jax.ShapeDtypeStruct((batch, ML_HEADS, ML_DH, ML_DH), F32),
            jax.ShapeDtypeStruct((batch, ML_HEADS, ML_DH), F32),
            jax.ShapeDtypeStruct((batch, 1, ML_HEADS), F32),
            jax.ShapeDtypeStruct((batch, CONV_BUF, D_MODEL), F32),
        ],
        scratch_shapes=[pltpu.VMEM((tt, D_MODEL), BF16), pltpu.VMEM((len(FUSED_COLS), tt, D_MODEL), BF16),
                        pltpu.VMEM((N_BRANCH, tt, D_MODEL), BF16), pltpu.VMEM((N_BRANCH, tt, D_MODEL), BF16)]
                       + _conv_scratch(tt)
                       + [pltpu.VMEM((MEM_LEN, D_MODEL), BF16), pltpu.VMEM((MEM_LEN, D_MODEL), BF16)],
        compiler_params=_params(("parallel", "arbitrary")),
        name="branches",
    )(x2d, wts["g_mix_pre"], wts["w_lo"], wts["w_hi"], wts["w_gate"], wts["w_gate_t"], wts["bias_row"], wts["bias_col"],
      wts["g_mhead"], wts["conv_w"], wts["conv_b"], wts["ln_g"], wts["ln_b"], mk, mv,
      wts["w_branch"], wts["w_out"], wts["g_mix_post"])
    x_mid, c, n, mm, conv_out = outs
    return x_mid, (c, n, mm.reshape(batch, ML_HEADS)), conv_out


def _inproj_kernel(x_ref, g_ref, wlo_ref, whi_ref, wg_ref, wgt_ref, p_ref, go_ref, gto_ref):
    xn = _rms(x_ref[...], g_ref[...]).astype(BF16)
    for c in range(N_MAIN // D_MODEL):
        p_ref[:, _col(c)] = jnp.dot(xn, _w_chunk(wlo_ref, whi_ref, c), preferred_element_type=F32).astype(BF16)
    go_ref[...] = jnp.dot(xn, wg_ref[...], preferred_element_type=F32)
    gto_ref[...] = lax.dot_general(wgt_ref[...], xn, NT_DIMS, preferred_element_type=F32)


def _inproj(x2d, wts, l, *, batch, seq):
    m = x2d.shape[0]
    p, go, gto = pl.pallas_call(
        _inproj_kernel,
        grid=(1,),
        in_specs=[
            pl.BlockSpec((m, D_MODEL), lambda i: (0, 0)),
            _layer_vec(l), *_w_specs(l),
            _resident((None, D_MODEL, GATE_PAD), lambda i: (l, 0, 0)),
            _resident((None, GATE_ROWS, D_MODEL), lambda i: (l, 0, 0)),
        ],
        out_specs=[
            pl.BlockSpec((m, N_MAIN), lambda i: (0, 0)),
            pl.BlockSpec((m, GATE_PAD), lambda i: (0, 0)),
            pl.BlockSpec((GATE_ROWS, m), lambda i: (0, 0)),
        ],
        out_shape=[
            jax.ShapeDtypeStruct((m, N_MAIN), BF16),
            jax.ShapeDtypeStruct((m, GATE_PAD), F32),
            jax.ShapeDtypeStruct((GATE_ROWS, m), F32),
        ],
        compiler_params=_params(("arbitrary",)),
        name="inproj",
    )(x2d, wts["g_mix_pre"], wts["w_lo"], wts["w_hi"], wts["w_gate"], wts["w_gate_t"])
    return p, go, jnp.transpose(gto.reshape(GATE_ROWS, batch, seq), (1, 0, 2))


def _mlstm_kernel(q_ref, k_ref, v_ref, o_ref, g_ref, gt_ref, brow_ref, bcol_ref, gh_ref, c0_ref, n0_ref, m0_ref,
                  y_ref, c_ref, n_ref, m_ref, *, L):
    @pl.when(pl.program_id(1) == 0)
    def _():
        c_ref[...] = c0_ref[...]
        n_ref[...] = n0_ref[...]
        m_ref[...] = m0_ref[...]

    _mlstm_body((q_ref, 0), (k_ref, 0), (v_ref, 0), (o_ref, 0),
                g_ref[...] + brow_ref[...], gt_ref[...] + bcol_ref[...], gh_ref,
                y_ref, c_ref, n_ref, m_ref, L=L)


def _mlstm(p, go, gto, wts, l, state, *, batch, seq, L):
    nb = seq // L
    m = batch * seq
    rows = lambda b, t: b * nb + t
    c0, n0, m0 = state
    y, c, n, mm = pl.pallas_call(
        functools.partial(_mlstm_kernel, L=L),
        grid=(batch, nb),
        in_specs=[
            pl.BlockSpec((L, D_MODEL), lambda b, t: (rows(b, t), COL_Q)),
            pl.BlockSpec((L, D_MODEL), lambda b, t: (rows(b, t), COL_K)),
            pl.BlockSpec((L, D_MODEL), lambda b, t: (rows(b, t), COL_V)),
            pl.BlockSpec((L, D_MODEL), lambda b, t: (rows(b, t), COL_O)),
            pl.BlockSpec((L, GATE_PAD), lambda b, t: (rows(b, t), 0)),
            pl.BlockSpec((None, GATE_ROWS, L), lambda b, t: (b, 0, t)),
            _layer_vec(l, GATE_PAD),
            pl.BlockSpec((None, GATE_ROWS, 1), lambda b, t: (l, 0, 0)),
            _layer_vec(l),
            pl.BlockSpec((None, None, ML_HEADS, ML_DH, ML_DH), lambda b, t: (l, b, 0, 0, 0)),
            pl.BlockSpec((None, None, ML_HEADS, ML_DH), lambda b, t: (l, b, 0, 0)),
            pl.BlockSpec((None, None, 1, ML_HEADS), lambda b, t: (l, b, 0, 0)),
        ],
        out_specs=[
            pl.BlockSpec((L, D_MODEL), lambda b, t: (rows(b, t), 0)),
            pl.BlockSpec((None, ML_HEADS, ML_DH, ML_DH), lambda b, t: (b, 0, 0, 0)),
            pl.BlockSpec((None, ML_HEADS, ML_DH), lambda b, t: (b, 0, 0)),
            pl.BlockSpec((None, 1, ML_HEADS), lambda b, t: (b, 0, 0)),
        ],
        out_shape=[
            jax.ShapeDtypeStruct((m, D_MODEL), BF16),
            jax.ShapeDtypeStruct((batch, ML_HEADS, ML_DH, ML_DH), F32),
            jax.ShapeDtypeStruct((batch, ML_HEADS, ML_DH), F32),
            jax.ShapeDtypeStruct((batch, 1, ML_HEADS), F32),
        ],
        compiler_params=_params(("parallel", "arbitrary")),
        name="mlstm",
    )(p, p, p, p, go, gto, wts["bias_row"], wts["bias_col"], wts["g_mhead"], c0, n0, m0)
    return y, (c, n, mm.reshape(batch, ML_HEADS))


def _conv_kernel(ga_ref, gb_ref, w_ref, cb_ref, lng_ref, lnb_ref, s0_ref, y_ref, so_ref, ext_ref, out_ref, *, tt):
    @pl.when(pl.program_id(1) == 0)
    def _():
        _conv_init(ext_ref, s0_ref)

    u = ga_ref[...].astype(F32) * _sigmoid(gb_ref[...].astype(F32))
    _conv_store(u, ext_ref, tt=tt)
    for c in range(N_SLABS):
        _conv_taps(c, w_ref, cb_ref, ext_ref, out_ref, tt=tt)
    _conv_finish(lng_ref, lnb_ref, y_ref, so_ref, ext_ref, out_ref, tt=tt)


def _conv(p, wts, l, state, *, batch, seq, tt):
    nt = seq // tt
    m = batch * seq
    rows = lambda b, t: b * nt + t
    return pl.pallas_call(
        functools.partial(_conv_kernel, tt=tt),
        grid=(batch, nt),
        in_specs=[
            pl.BlockSpec((tt, D_MODEL), lambda b, t: (rows(b, t), COL_GA)),
            pl.BlockSpec((tt, D_MODEL), lambda b, t: (rows(b, t), COL_GB)),
            pl.BlockSpec((None, CONV_W, N_SLABS, SUBLANES, LANES), lambda b, t: (l, 0, 0, 0, 0)),
            pl.BlockSpec((None, N_SLABS, SUBLANES, LANES), lambda b, t: (l, 0, 0, 0)),
            _layer_vec(l), _layer_vec(l),
            pl.BlockSpec((None, None, CONV_BUF, D_MODEL), lambda b, t: (l, b, 0, 0)),
        ],
        out_specs=[pl.BlockSpec((tt, D_MODEL), lambda b, t: (rows(b, t), 0)),
                   pl.BlockSpec((None, CONV_BUF, D_MODEL), lambda b, t: (b, 0, 0))],
        out_shape=[
            jax.ShapeDtypeStruct((m, D_MODEL), BF16),
            jax.ShapeDtypeStruct((batch, CONV_BUF, D_MODEL), F32),
        ],
        scratch_shapes=_conv_scratch(tt),
        compiler_params=_params(("parallel", "arbitrary")),
        name="conv",
    )(p, p, wts["conv_w"], wts["conv_b"], wts["ln_g"], wts["ln_b"], state)


def _xattn_kernel(q_ref, k_ref, v_ref, y_ref, kb_ref, vb_ref):
    @pl.when(pl.program_id(1) == 0)
    def _():
        kb_ref[...] = k_ref[...].astype(BF16)
        vb_ref[...] = v_ref[...].astype(BF16)

    _xattn_body((q_ref, 0), kb_ref, vb_ref, y_ref)


def _xattn(p, mk, mv, l, *, batch, seq, tt):
    nt = seq // tt
    m = batch * seq
    kv_spec = pl.BlockSpec((None, None, MEM_LEN, D_MODEL), lambda b, t: (l, b, 0, 0))
    return pl.pallas_call(
        _xattn_kernel,
        grid=(batch, nt),
        in_specs=[pl.BlockSpec((tt, D_MODEL), lambda b, t: (b * nt + t, COL_XQ)), kv_spec, kv_spec],
        out_specs=pl.BlockSpec((tt, D_MODEL), lambda b, t: (b * nt + t, 0)),
        out_shape=jax.ShapeDtypeStruct((m, D_MODEL), BF16),
        scratch_shapes=[pltpu.VMEM((MEM_LEN, D_MODEL), BF16), pltpu.VMEM((MEM_LEN, D_MODEL), BF16)],
        compiler_params=_params(("parallel", "arbitrary")),
        name="xattn",
    )(p, mk, mv)


def _merge_kernel(x_ref, yml_ref, ycv_ref, yxa_ref, g0_ref, g1_ref, g2_ref, wb_ref, wo_ref, gp_ref, o_ref):
    mixed = None
    for i, (y_ref, gate_ref) in enumerate(((yml_ref, g0_ref), (ycv_ref, g1_ref), (yxa_ref, g2_ref))):
        term = _sigmoid(gate_ref[...].astype(F32)) * jnp.dot(y_ref[...], wb_ref[i], preferred_element_type=F32)
        mixed = term if mixed is None else mixed + term
    z = jnp.dot(mixed.astype(BF16), wo_ref[...], preferred_element_type=F32)
    o_ref[...] = x_ref[...] + _rms(z, gp_ref[...])


def _merge(x2d, y_ml, y_cv, y_xa, gates, gate_col, wts, l, *, tm):
    m = x2d.shape[0]
    tile = lambda c: pl.BlockSpec((tm, D_MODEL), lambda i: (i, c))
    return pl.pallas_call(
        _merge_kernel,
        grid=(m // tm,),
        in_specs=[
            tile(0), tile(0), tile(0), tile(0),
            tile(gate_col), tile(gate_col + 1), tile(gate_col + 2),
            _resident((None, N_BRANCH, D_MODEL, D_MODEL), lambda i: (l, 0, 0, 0)),
            _resident((None, D_MODEL, D_MODEL), lambda i: (l, 0, 0)),
            _layer_vec(l),
        ],
        out_specs=tile(0),
        out_shape=jax.ShapeDtypeStruct((m, D_MODEL), F32),
        compiler_params=_params(("parallel",)),
        name="merge",
    )(x2d, y_ml, y_cv, y_xa, gates, gates, gates, wts["w_branch"], wts["w_out"], wts["g_mix_post"])


FFN_CHUNKS = ((0, 768), (768, 768), (1536, 768), (2304, 512))


def _ffn_kernel(x_ref, gpre_ref, wa_ref, wb_ref, wo_ref, gpost_ref, o_ref, h_ref):
    x = x_ref[...]
    xn = _rms(x, gpre_ref[...]).astype(BF16)
    for start, size in FFN_CHUNKS:
        sl = slice(start, start + size)
        fa = jnp.dot(xn, wa_ref[:, sl], preferred_element_type=F32)
        fb = jnp.dot(xn, wb_ref[:, sl], preferred_element_type=F32)
        h_ref[:, sl] = (fa * _sigmoid(fa) * fb).astype(BF16)
    z = jnp.dot(h_ref[...], wo_ref[...], preferred_element_type=F32)
    o_ref[...] = x + _rms(z, gpost_ref[...])


def _ffn(x2d, wts, l, *, tm):
    m = x2d.shape[0]
    return pl.pallas_call(
        _ffn_kernel,
        grid=(m // tm,),
        in_specs=[
            pl.BlockSpec((tm, D_MODEL), lambda i: (i, 0)),
            _layer_vec(l),
            _resident((None, D_MODEL, D_FF), lambda i: (l, 0, 0)),
            _resident((None, D_MODEL, D_FF), lambda i: (l, 0, 1)),
            _resident((None, D_FF, D_MODEL), lambda i: (l, 0, 0)),
            _layer_vec(l),
        ],
        out_specs=pl.BlockSpec((tm, D_MODEL), lambda i: (i, 0)),
        out_shape=jax.ShapeDtypeStruct((m, D_MODEL), F32),
        scratch_shapes=[pltpu.VMEM((tm, D_FF), BF16)],
        compiler_params=_params(("parallel",)),
        name="ffn",
    )(x2d, wts["g_ffn_pre"], wts["w_ffn_in"], wts["w_ffn_in"], wts["w_ffn_out"], wts["g_ffn_post"])


def _cast_columns_kernel(a_ref, b_ref, o_ref, *, shift):
    if shift == 0:
        o_ref[...] = a_ref[...].astype(BF16)
    else:
        wide = jnp.concatenate([a_ref[...], b_ref[...]], axis=1)
        o_ref[...] = wide[:, shift:shift + D_MODEL].astype(BF16)


def _cast_columns(w, first_chunk, n_chunks, shift, *, tk=512):
    depth, k, _ = w.shape
    tail = D_MODEL // LANES
    return pl.pallas_call(
        functools.partial(_cast_columns_kernel, shift=shift),
        grid=(depth, k // tk, n_chunks),
        in_specs=[
            pl.BlockSpec((None, tk, D_MODEL), lambda l, i, j: (l, i, first_chunk + j)),
            pl.BlockSpec((None, tk, LANES), lambda l, i, j: (l, i, (first_chunk + j + 1) * tail)),
        ],
        out_specs=pl.BlockSpec((None, tk, D_MODEL), lambda l, i, j: (l, i, j)),
        out_shape=jax.ShapeDtypeStruct((depth, k, n_chunks * D_MODEL), BF16),
        compiler_params=_params(("parallel", "parallel", "parallel")),
        name="cast_columns",
    )(w, w)


def _memkv_kernel(x_ref, g_ref, w_ref, k_ref, v_ref):
    xn = _rms(x_ref[...], g_ref[...]).astype(BF16)
    k_ref[...] = jnp.dot(xn, w_ref[:, :D_MODEL], preferred_element_type=F32)
    v_ref[...] = jnp.dot(xn, w_ref[:, D_MODEL:], preferred_element_type=F32)


def _memkv(mem2d, wts, l, *, tm):
    m = mem2d.shape[0]
    out = jax.ShapeDtypeStruct((m, D_MODEL), F32)
    tile = pl.BlockSpec((tm, D_MODEL), lambda i: (i, 0))
    return pl.pallas_call(
        _memkv_kernel,
        grid=(m // tm,),
        in_specs=[tile, _layer_vec(l), _resident((None, D_MODEL, 2 * D_MODEL), lambda i: (l, 0, 0))],
        out_specs=[tile, tile],
        out_shape=[out, out],
        compiler_params=_params(("parallel",)),
        name="memkv",
    )(mem2d, wts["g_mem"], wts["w_mem_kv"])


def _layer_first_chunks(x2d, wts, l, mk, mv, *, batch, seq):
    tiles = _tiles(batch, seq)
    x2d, ml_state, conv_out = _branches(x2d, wts, l, mk, mv, batch=batch, seq=seq, tt=tiles["L"])
    x2d = _ffn(x2d, wts, l, tm=tiles["tm"])
    return x2d, ml_state, conv_out


def _layer_later_chunk(x2d, wts, l, ml_state, conv_state, mk, mv, *, batch, seq):
    tiles = _tiles(batch, seq)
    p, go, gto = _inproj(x2d, wts, l, batch=batch, seq=seq)
    y_ml, ml_state = _mlstm(p, go, gto, wts, l, ml_state, batch=batch, seq=seq, L=tiles["L"])
    y_cv, conv_out = _conv(p, wts, l, conv_state, batch=batch, seq=seq, tt=tiles["L"])
    y_xa = _xattn(p, mk, mv, l, batch=batch, seq=seq, tt=tiles["tt_xa"])
    x2d = _merge(x2d, y_ml, y_cv, y_xa, p, COL_GATE, wts, l, tm=tiles["tm"])
    x2d = _ffn(x2d, wts, l, tm=tiles["tm"])
    return x2d, ml_state, conv_out


def kernel(x_prompt, x_sample, mem_prompt, state_mlstm_c, state_mlstm_n, state_mlstm_m, state_conv,
           cache_mem_k, cache_mem_v, norm_mix_pre, norm_mix_post, norm_ffn_pre, norm_ffn_post, w_in, b_gate,
           mlstm_head_norm, conv_w, conv_b, conv_ln_g, conv_ln_b, mem_norm, w_mem_kv, w_branch, w_out,
           w_ffn_in, w_ffn_out):
    depth = w_in.shape[0]
    bp, sp, _ = x_prompt.shape
    bs, ss, _ = x_sample.shape
    gate_lo = 4 * D_MODEL
    gate_hi = gate_lo + N_GATES

    vec = lambda v: v.reshape(depth, 1, -1).astype(F32)
    w_gate_cols = w_in[:, :, gate_lo:gate_hi]
    wts = {
        "w_lo": _cast_columns(w_in, 0, N_LO, 0), "w_hi": _cast_columns(w_in, N_LO, N_CHUNKS - N_LO, N_GATES),
        "w_gate": jnp.pad(w_gate_cols, ((0, 0), (0, 0), (0, GATE_PAD - N_GATES))).astype(BF16),
        "w_gate_t": jnp.pad(jnp.swapaxes(w_gate_cols, 1, 2), ((0, 0), (0, GATE_ROWS - N_GATES), (0, 0))).astype(BF16),
        "bias_row": jnp.pad(b_gate, ((0, 0), (0, GATE_PAD - N_GATES))).reshape(depth, 1, GATE_PAD).astype(F32),
        "bias_col": jnp.pad(b_gate, ((0, 0), (0, GATE_ROWS - N_GATES))).reshape(depth, GATE_ROWS, 1).astype(F32),
        "g_mix_pre": vec(norm_mix_pre), "g_mix_post": vec(norm_mix_post),
        "g_ffn_pre": vec(norm_ffn_pre), "g_ffn_post": vec(norm_ffn_post),
        "g_mhead": vec(mlstm_head_norm), "g_mem": vec(mem_norm),
        "conv_w": jnp.broadcast_to(conv_w.astype(F32).reshape(depth, CONV_W, N_SLABS, 1, LANES),
                                   (depth, CONV_W, N_SLABS, SUBLANES, LANES)),
        "conv_b": jnp.broadcast_to(conv_b.astype(F32).reshape(depth, N_SLABS, 1, LANES),
                                   (depth, N_SLABS, SUBLANES, LANES)),
        "ln_g": vec(conv_ln_g), "ln_b": vec(conv_ln_b),
        "w_branch": w_branch.astype(BF16), "w_out": w_out.astype(BF16),
        "w_ffn_in": w_ffn_in.astype(BF16), "w_ffn_out": w_ffn_out.astype(BF16),
        "w_mem_kv": w_mem_kv.astype(BF16),
    }
    s_state = (state_mlstm_c.astype(F32), state_mlstm_n.astype(F32),
               state_mlstm_m.astype(F32).reshape(depth, bs, 1, ML_HEADS))
    s_conv = state_conv.astype(F32)
    cache_k = cache_mem_k.reshape(depth, bs, MEM_LEN, D_MODEL)
    cache_v = cache_mem_v.reshape(depth, bs, MEM_LEN, D_MODEL)

    yp = x_prompt.reshape(bp * sp, D_MODEL)
    ys = x_sample.reshape(bs * ss, D_MODEL)
    mem2d = mem_prompt.reshape(bp * MEM_LEN, D_MODEL)

    pc, pn, pm, pconv, pmk, pmv = [], [], [], [], [], []
    sc, sn, sm, sconv = [], [], [], []
    for l in range(depth):
        mk_p, mv_p = _memkv(mem2d, wts, l, tm=256)
        yp, (c1, n1, m1), buf1 = _layer_first_chunks(
            yp, wts, l, mk_p.reshape(bp, MEM_LEN, D_MODEL), mv_p.reshape(bp, MEM_LEN, D_MODEL), batch=bp, seq=sp)
        ys, (c2, n2, m2), buf2 = _layer_later_chunk(ys, wts, l, s_state, s_conv, cache_k, cache_v, batch=bs, seq=ss)
        pc.append(c1); pn.append(n1); pm.append(m1); pconv.append(buf1)
        pmk.append(mk_p.reshape(bp, MEM_LEN, XA_HEADS, XA_DH)); pmv.append(mv_p.reshape(bp, MEM_LEN, XA_HEADS, XA_DH))
        sc.append(c2); sn.append(n2); sm.append(m2); sconv.append(buf2)
    return (yp.reshape(bp, sp, D_MODEL), ys.reshape(bs, ss, D_MODEL),
            jnp.stack(pc), jnp.stack(pn), jnp.stack(pm), jnp.stack(pconv), jnp.stack(pmk), jnp.stack(pmv),
            jnp.stack(sc), jnp.stack(sn), jnp.stack(sm), jnp.stack(sconv))
```

```python
import functools

import jax
import jax.numpy as jnp
from jax import lax
from jax.experimental import pallas as pl
from jax.experimental.pallas import tpu as pltpu

F32 = jnp.float32
BF16 = jnp.bfloat16

D_MODEL = 1024
ML_HEADS = 4
ML_DH = 256
XA_HEADS = 4
XA_DH = 256
MEM_LEN = 256
CONV_W = 31
CONV_BUF = CONV_W - 1
D_FF = 2816
RMS_EPS = 1e-6
LN_EPS = 1e-5

N_MAIN = 10 * D_MODEL
COL_Q, COL_K, COL_V, COL_O, COL_GA, COL_GB, COL_XQ, COL_GATE = 0, 1, 2, 3, 4, 5, 6, 7
N_BRANCH = 3
N_GATES = 2 * ML_HEADS
GATE_PAD = 128
GATE_ROWS = 16
FUSED_COLS = (COL_Q, COL_K, COL_V, COL_O, COL_XQ)

V7X_VMEM_LIMIT_BYTES = 56 * 1024 * 1024
SUBLANES = 8
LANES = 128
N_SLABS = D_MODEL // LANES
N_CHUNKS = N_MAIN // D_MODEL
N_LO = COL_O + 1


def _w_chunk(w_lo_ref, w_hi_ref, c):
    return w_lo_ref[:, _col(c)] if c < N_LO else w_hi_ref[:, _col(c - N_LO)]


def _w_specs(l):
    return [_resident((None, D_MODEL, N_LO * D_MODEL), lambda *_: (l, 0, 0)),
            _resident((None, D_MODEL, (N_CHUNKS - N_LO) * D_MODEL), lambda *_: (l, 0, 0))]
CONV_PAD = 32
CONV_POS_GROUP = 12

NT_DIMS = (((1,), (1,)), ((), ()))
TN_DIMS = (((0,), (0,)), ((), ()))

_sigmoid = jax.nn.sigmoid


def _tiles(batch, seq):
    if seq >= 512:
        return dict(tm=512, L=256, tt_xa=512)
    return dict(tm=batch * seq, L=seq, tt_xa=seq)


def _resident(block_shape, index_map):
    return pl.BlockSpec(block_shape, index_map, pipeline_mode=pl.Buffered(1))


def _layer_vec(l, width=D_MODEL):
    return pl.BlockSpec((None, 1, width), lambda *_: (l, 0, 0))


def _params(semantics):
    return pltpu.CompilerParams(dimension_semantics=semantics, vmem_limit_bytes=V7X_VMEM_LIMIT_BYTES)


def _col(c):
    return slice(c * D_MODEL, (c + 1) * D_MODEL)


def _rms(x, g):
    return x * lax.rsqrt(jnp.mean(x * x, axis=-1, keepdims=True) + RMS_EPS) * g


def _mlstm_body(q_src, k_src, v_src, o_src, gates_c, gates_r, gh_ref, y_ref, c_ref, n_ref, m_ref, *, L):
    row = lax.broadcasted_iota(jnp.int32, (L, L), 0)
    col = lax.broadcasted_iota(jnp.int32, (L, L), 1)
    causal = col <= row
    tril = causal.astype(BF16)
    triu = (row <= col).astype(BF16)

    def split3(x):
        hi = x.astype(BF16)
        r = x - hi.astype(F32)
        mid = r.astype(BF16)
        return hi, mid, (r - mid.astype(F32)).astype(BF16)

    cum_c = sum(jnp.dot(tril, part, preferred_element_type=F32)
                for part in split3(jax.nn.log_sigmoid(gates_c)))
    cum_r = sum(jnp.dot(part, triu, preferred_element_type=F32)
                for part in split3(jax.nn.log_sigmoid(gates_r)))

    def head_cols(src, h):
        ref, first = src
        return ref[:, first + h * ML_DH:first + (h + 1) * ML_DH]

    H = range(ML_HEADS)
    hs = [slice(h * ML_DH, (h + 1) * ML_DH) for h in H]
    ig_c = [gates_c[:, h:h + 1] for h in H]
    b_c = [cum_c[:, ML_HEADS + h:ML_HEADS + h + 1] for h in H]
    src_r = [cum_r[ML_HEADS + h:ML_HEADS + h + 1, :] - gates_r[h:h + 1, :] for h in H]
    m_prev = [m_ref[:, h:h + 1] for h in H]

    d = [jnp.where(causal, b_c[h] - src_r[h], -jnp.inf) for h in H]
    a = [b_c[h] + m_prev[h] for h in H]
    m_row = [jnp.maximum(a[h], jnp.max(d[h], axis=1, keepdims=True)) for h in H]

    q = [head_cols(q_src, h) for h in H]
    ks = [head_cols(k_src, h) * jnp.asarray(ML_DH ** -0.5, BF16) for h in H]
    v = [head_cols(v_src, h) for h in H]
    qk = [lax.dot_general(q[h], ks[h], NT_DIMS, preferred_element_type=F32) for h in H]
    s = [qk[h] * jnp.exp(d[h] - m_row[h]) for h in H]
    w_inter = [jnp.exp(a[h] - m_row[h]) for h in H]

    c = [c_ref[h] for h in H]
    n = [n_ref[h:h + 1, :] for h in H]
    qc = [lax.dot_general(q[h], c[h].astype(BF16), NT_DIMS, preferred_element_type=F32) for h in H]
    sv = [jnp.dot(s[h].astype(BF16), v[h], preferred_element_type=F32) for h in H]
    num = [w_inter[h] * qc[h] + sv[h] for h in H]
    qn = [jnp.sum(q[h].astype(F32) * n[h], axis=1, keepdims=True) for h in H]
    den = [w_inter[h] * qn[h] + jnp.sum(s[h], axis=1, keepdims=True) for h in H]
    hh = [num[h] * (1.0 / jnp.maximum(jnp.abs(den[h]), jnp.exp(-m_row[h]))) for h in H]

    b_last = [b_c[h][L - 1:L, :] for h in H]
    g_c = [b_last[h] - b_c[h] + ig_c[h] for h in H]
    m_new = [jnp.maximum(b_last[h] + m_prev[h], jnp.max(g_c[h], axis=0, keepdims=True)) for h in H]
    w_old = [jnp.exp(b_last[h] + m_prev[h] - m_new[h]) for h in H]
    w_s = [jnp.exp(g_c[h] - m_new[h]) for h in H]
    wv = [(w_s[h] * v[h].astype(F32)).astype(BF16) for h in H]
    for h in H:
        c_ref[h] = w_old[h] * c[h] + lax.dot_general(wv[h], ks[h], TN_DIMS, preferred_element_type=F32)
        n_ref[h:h + 1, :] = w_old[h] * n[h] + jnp.sum(w_s[h] * ks[h].astype(F32), axis=0, keepdims=True)
        m_ref[:, h:h + 1] = m_new[h]

    hn = [hh[h] * lax.rsqrt(jnp.mean(hh[h] * hh[h], axis=1, keepdims=True) + RMS_EPS) * gh_ref[:, hs[h]] for h in H]
    for h in H:
        y_ref[:, hs[h]] = (hn[h] * _sigmoid(head_cols(o_src, h).astype(F32))).astype(BF16)


def _conv_init(ext_ref, s0_ref):
    ext_ref[...] = jnp.zeros_like(ext_ref)
    if s0_ref is not None:
        for c in range(N_SLABS):
            ext_ref[c, CONV_PAD - CONV_BUF:CONV_PAD, :] = s0_ref[:, c * LANES:(c + 1) * LANES]


def _conv_store(u, ext_ref, *, tt):
    for c in range(N_SLABS):
        ext_ref[c, CONV_PAD:CONV_PAD + tt, :] = u[:, c * LANES:(c + 1) * LANES]


def _conv_taps(c, w_ref, cb_ref, ext_ref, out_ref, *, tt):
    seg = (tt + CONV_PAD) // SUBLANES
    bias = cb_ref[c]
    for p0 in range(0, seg, CONV_POS_GROUP):
        p1 = min(p0 + CONV_POS_GROUP, seg)
        acc = [None] * (p1 - p0)
        for q in range(p0, p1 + CONV_W - 1):
            z = ext_ref[c, pl.ds(q, SUBLANES, stride=seg), :]
            for p in range(max(p0, q - CONV_W + 1), min(p1, q + 1)):
                term = z * w_ref[q - p, c]
                acc[p - p0] = term if acc[p - p0] is None else acc[p - p0] + term
        for p in range(p0, p1):
            out_ref[c, pl.ds(p + CONV_BUF, SUBLANES, stride=seg), :] = acc[p - p0] + bias


def _conv_finish(lng_ref, lnb_ref, y_ref, so_ref, ext_ref, out_ref, *, tt):
    lo = CONV_PAD - CONV_BUF
    for c in range(N_SLABS):
        so_ref[:, c * LANES:(c + 1) * LANES] = ext_ref[c, tt + lo:tt + CONV_PAD, :]
        ext_ref[c, 0:CONV_PAD, :] = ext_ref[c, tt:tt + CONV_PAD, :]

    x = jnp.concatenate([out_ref[c, CONV_PAD:CONV_PAD + tt, :] for c in range(N_SLABS)], axis=1)
    mu = jnp.mean(x, axis=-1, keepdims=True)
    xc = x - mu
    yn = xc * lax.rsqrt(jnp.mean(xc * xc, axis=-1, keepdims=True) + LN_EPS) * lng_ref[...] + lnb_ref[...]
    y_ref[...] = (yn * _sigmoid(yn)).astype(BF16)


def _conv_scratch(tt):
    slab = pltpu.VMEM((N_SLABS, tt + 2 * CONV_PAD, LANES), F32)
    return [slab, slab]


def _xattn_body(q_src, kb_ref, vb_ref, y_ref):
    q_ref, first = q_src
    for h in range(XA_HEADS):
        hs = slice(h * XA_DH, (h + 1) * XA_DH)
        q = q_ref[:, first + h * XA_DH:first + (h + 1) * XA_DH]
        s = lax.dot_general(q, kb_ref[:, hs], NT_DIMS, preferred_element_type=F32)
        s = s * (XA_DH ** -0.5)
        e = jnp.exp(s - jnp.max(s, axis=-1, keepdims=True))
        p = e * (1.0 / jnp.sum(e, axis=-1, keepdims=True))
        y_ref[:, hs] = jnp.dot(p.astype(BF16), vb_ref[:, hs], preferred_element_type=F32).astype(BF16)


def _branches_kernel(x_ref, gpre_ref, wlo_ref, whi_ref, wg_ref, wgt_ref, brow_ref, bcol_ref, gh_ref,
                     cw_ref, cb_ref, lng_ref, lnb_ref, mk_ref, mv_ref, wb_ref, wo_ref, gpost_ref,
                     xo_ref, c_ref, n_ref, m_ref, so_ref,
                     xn_scr, p_scr, y_scr, g_scr, ext_ref, out_ref, kb_ref, vb_ref, *, tt):
    @pl.when(pl.program_id(1) == 0)
    def _():
        c_ref[...] = jnp.zeros_like(c_ref)
        n_ref[...] = jnp.zeros_like(n_ref)
        m_ref[...] = jnp.zeros_like(m_ref)
        _conv_init(ext_ref, None)
        kb_ref[...] = mk_ref[...].astype(BF16)
        vb_ref[...] = mv_ref[...].astype(BF16)

    xn_scr[...] = _rms(x_ref[...], gpre_ref[...]).astype(BF16)
    proj = lambda c: jnp.dot(xn_scr[...], _w_chunk(wlo_ref, whi_ref, c), preferred_element_type=F32)
    _conv_store(proj(COL_GA) * _sigmoid(proj(COL_GB)), ext_ref, tt=tt)

    for j, c in enumerate(FUSED_COLS):
        p_scr[j] = proj(c).astype(BF16)
    for c in range(N_SLABS):
        _conv_taps(c, cw_ref, cb_ref, ext_ref, out_ref, tt=tt)
    for j in range(N_BRANCH):
        g_scr[j] = proj(COL_GATE + j).astype(BF16)
    xn = xn_scr[...]
    gates_c = jnp.dot(xn, wg_ref[...], preferred_element_type=F32) + brow_ref[...]
    gates_r = lax.dot_general(wgt_ref[...], xn, NT_DIMS, preferred_element_type=F32) + bcol_ref[...]

    yml_ref, ycv_ref, yxa_ref = (y_scr.at[j] for j in range(N_BRANCH))
    _conv_finish(lng_ref, lnb_ref, ycv_ref, so_ref, ext_ref, out_ref, tt=tt)
    src = lambda c: (p_scr.at[FUSED_COLS.index(c)], 0)
    _mlstm_body(src(COL_Q), src(COL_K), src(COL_V), src(COL_O), gates_c, gates_r, gh_ref,
                yml_ref, c_ref, n_ref, m_ref, L=tt)
    _xattn_body(src(COL_XQ), kb_ref, vb_ref, yxa_ref)
    _merge_kernel(x_ref, yml_ref, ycv_ref, yxa_ref, g_scr.at[0], g_scr.at[1], g_scr.at[2],
                  wb_ref, wo_ref, gpost_ref, xo_ref)


def _branches(x2d, wts, l, mk, mv, *, batch, seq, tt):
    nt = seq // tt
    m = batch * seq
    rows = lambda b, t: (b * nt + t, 0)
    tile = lambda width: pl.BlockSpec((tt, width), rows)
    kv_spec = pl.BlockSpec((None, MEM_LEN, D_MODEL), lambda b, t: (b, 0, 0))
    outs = pl.pallas_call(
        functools.partial(_branches_kernel, tt=tt),
        grid=(batch, nt),
        in_specs=[
            tile(D_MODEL), _layer_vec(l), *_w_specs(l),
            _resident((None, D_MODEL, GATE_PAD), lambda b, t: (l, 0, 0)),
            _resident((None, GATE_ROWS, D_MODEL), lambda b, t: (l, 0, 0)),
            _layer_vec(l, GATE_PAD),
            pl.BlockSpec((None, GATE_ROWS, 1), lambda b, t: (l, 0, 0)),
            _layer_vec(l),
            pl.BlockSpec((None, CONV_W, N_SLABS, SUBLANES, LANES), lambda b, t: (l, 0, 0, 0, 0)),
            pl.BlockSpec((None, N_SLABS, SUBLANES, LANES), lambda b, t: (l, 0, 0, 0)),
            _layer_vec(l), _layer_vec(l),
            kv_spec, kv_spec,
            _resident((None, N_BRANCH, D_MODEL, D_MODEL), lambda b, t: (l, 0, 0, 0)),
            _resident((None, D_MODEL, D_MODEL), lambda b, t: (l, 0, 0)),
            _layer_vec(l),
        ],
        out_specs=[
            tile(D_MODEL),
            pl.BlockSpec((None, ML_HEADS, ML_DH, ML_DH), lambda b, t: (b, 0, 0, 0)),
            pl.BlockSpec((None, ML_HEADS, ML_DH), lambda b, t: (b, 0, 0)),
            pl.BlockSpec((None, 1, ML_HEADS), lambda b, t: (b, 0, 0)),
            pl.BlockSpec((None, CONV_BUF, D_MODEL), lambda b, t: (b, 0, 0)),
        ],
        out_shape=[
            jax.ShapeDtypeStruct((m, D_MODEL), F32),
            jax.ShapeDtypeStruct((batch, ML_HEADS, ML_DH, ML_DH), F32),
            jax.ShapeDtypeStruct((batch, ML_HEADS, ML_DH), F32),
            jax.ShapeDtypeStruct((batch, 1, ML_HEADS), F32),
            jax.ShapeDtypeStruct((batch, CONV_BUF, D_MODEL), F32),
        ],
        scratch_shapes=[pltpu.VMEM((tt, D_MODEL), BF16), pltpu.VMEM((len(FUSED_COLS), tt, D_MODEL), BF16),
                        pltpu.VMEM((N_BRANCH, tt, D_MODEL), BF16), pltpu.VMEM((N_BRANCH, tt, D_MODEL), BF16)]
                       + _conv_scratch(tt)
                       + [pltpu.VMEM((MEM_LEN, D_MODEL), BF16), pltpu.VMEM((MEM_LEN, D_MODEL), BF16)],
        compiler_params=_params(("parallel", "arbitrary")),
        name="branches",
    )(x2d, wts["g_mix_pre"], wts["w_lo"], wts["w_hi"], wts["w_gate"], wts["w_gate_t"], wts["bias_row"], wts["bias_col"],
      wts["g_mhead"], wts["conv_w"], wts["conv_b"], wts["ln_g"], wts["ln_b"], mk, mv,
      wts["w_branch"], wts["w_out"], wts["g_mix_post"])
    x_mid, c, n, mm, conv_out = outs
    return x_mid, (c, n, mm.reshape(batch, ML_HEADS)), conv_out


def _inproj_kernel(x_ref, g_ref, wlo_ref, whi_ref, wg_ref, wgt_ref, p_ref, go_ref, gto_ref):
    xn = _rms(x_ref[...], g_ref[...]).astype(BF16)
    for c in range(N_MAIN // D_MODEL):
        p_ref[:, _col(c)] = jnp.dot(xn, _w_chunk(wlo_ref, whi_ref, c), preferred_element_type=F32).astype(BF16)
    go_ref[...] = jnp.dot(xn, wg_ref[...], preferred_element_type=F32)
    gto_ref[...] = lax.dot_general(wgt_ref[...], xn, NT_DIMS, preferred_element_type=F32)


def _inproj(x2d, wts, l, *, batch, seq):
    m = x2d.shape[0]
    p, go, gto = pl.pallas_call(
        _inproj_kernel,
        grid=(1,),
        in_specs=[
            pl.BlockSpec((m, D_MODEL), lambda i: (0, 0)),
            _layer_vec(l), *_w_specs(l),
            _resident((None, D_MODEL, GATE_PAD), lambda i: (l, 0, 0)),
            _resident((None, GATE_ROWS, D_MODEL), lambda i: (l, 0, 0)),
        ],
        out_specs=[
            pl.BlockSpec((m, N_MAIN), lambda i: (0, 0)),
            pl.BlockSpec((m, GATE_PAD), lambda i: (0, 0)),
            pl.BlockSpec((GATE_ROWS, m), lambda i: (0, 0)),
        ],
        out_shape=[
            jax.ShapeDtypeStruct((m, N_MAIN), BF16),
            jax.ShapeDtypeStruct((m, GATE_PAD), F32),
            jax.ShapeDtypeStruct((GATE_ROWS, m), F32),
        ],
        compiler_params=_params(("arbitrary",)),
        name="inproj",
    )(x2d, wts["g_mix_pre"], wts["w_lo"], wts["w_hi"], wts["w_gate"], wts["w_gate_t"])
    return p, go, jnp.transpose(gto.reshape(GATE_ROWS, batch, seq), (1, 0, 2))


def _mlstm_kernel(q_ref, k_ref, v_ref, o_ref, g_ref, gt_ref, brow_ref, bcol_ref, gh_ref, c0_ref, n0_ref, m0_ref,
                  y_ref, c_ref, n_ref, m_ref, *, L):
    @pl.when(pl.program_id(1) == 0)
    def _():
        c_ref[...] = c0_ref[...]
        n_ref[...] = n0_ref[...]
        m_ref[...] = m0_ref[...]

    _mlstm_body((q_ref, 0), (k_ref, 0), (v_ref, 0), (o_ref, 0),
                g_ref[...] + brow_ref[...], gt_ref[...] + bcol_ref[...], gh_ref,
                y_ref, c_ref, n_ref, m_ref, L=L)


def _mlstm(p, go, gto, wts, l, state, *, batch, seq, L):
    nb = seq // L
    m = batch * seq
    rows = lambda b, t: b * nb + t
    c0, n0, m0 = state
    y, c, n, mm = pl.pallas_call(
        functools.partial(_mlstm_kernel, L=L),
        grid=(batch, nb),
        in_specs=[
            pl.BlockSpec((L, D_MODEL), lambda b, t: (rows(b, t), COL_Q)),
            pl.BlockSpec((L, D_MODEL), lambda b, t: (rows(b, t), COL_K)),
            pl.BlockSpec((L, D_MODEL), lambda b, t: (rows(b, t), COL_V)),
            pl.BlockSpec((L, D_MODEL), lambda b, t: (rows(b, t), COL_O)),
            pl.BlockSpec((L, GATE_PAD), lambda b, t: (rows(b, t), 0)),
            pl.BlockSpec((None, GATE_ROWS, L), lambda b, t: (b, 0, t)),
            _layer_vec(l, GATE_PAD),
            pl.BlockSpec((None, GATE_ROWS, 1), lambda b, t: (l, 0, 0)),
            _layer_vec(l),
            pl.BlockSpec((None, None, ML_HEADS, ML_DH, ML_DH), lambda b, t: (l, b, 0, 0, 0)),
            pl.BlockSpec((None, None, ML_HEADS, ML_DH), lambda b, t: (l, b, 0, 0)),
            pl.BlockSpec((None, None, 1, ML_HEADS), lambda b, t: (l, b, 0, 0)),
        ],
        out_specs=[
            pl.BlockSpec((L, D_MODEL), lambda b, t: (rows(b, t), 0)),
            pl.BlockSpec((None, ML_HEADS, ML_DH, ML_DH), lambda b, t: (b, 0, 0, 0)),
            pl.BlockSpec((None, ML_HEADS, ML_DH), lambda b, t: (b, 0, 0)),
            pl.BlockSpec((None, 1, ML_HEADS), lambda b, t: (b, 0, 0)),
        ],
        out_shape=[
            jax.ShapeDtypeStruct((m, D_MODEL), BF16),
            jax.ShapeDtypeStruct((batch, ML_HEADS, ML_DH, ML_DH), F32),
            jax.ShapeDtypeStruct((batch, ML_HEADS, ML_DH), F32),
            jax.ShapeDtypeStruct((batch, 1, ML_HEADS), F32),
        ],
        compiler_params=_params(("parallel", "arbitrary")),
        name="mlstm",
    )(p, p, p, p, go, gto, wts["bias_row"], wts["bias_col"], wts["g_mhead"], c0, n0, m0)
    return y, (c, n, mm.reshape(batch, ML_HEADS))


def _conv_kernel(ga_ref, gb_ref, w_ref, cb_ref, lng_ref, lnb_ref, s0_ref, y_ref, so_ref, ext_ref, out_ref, *, tt):
    @pl.when(pl.program_id(1) == 0)
    def _():
        _conv_init(ext_ref, s0_ref)

    u = ga_ref[...].astype(F32) * _sigmoid(gb_ref[...].astype(F32))
    _conv_store(u, ext_ref, tt=tt)
    for c in range(N_SLABS):
        _conv_taps(c, w_ref, cb_ref, ext_ref, out_ref, tt=tt)
    _conv_finish(lng_ref, lnb_ref, y_ref, so_ref, ext_ref, out_ref, tt=tt)


def _conv(p, wts, l, state, *, batch, seq, tt):
    nt = seq // tt
    m = batch * seq
    rows = lambda b, t: b * nt + t
    return pl.pallas_call(
        functools.partial(_conv_kernel, tt=tt),
        grid=(batch, nt),
        in_specs=[
            pl.BlockSpec((tt, D_MODEL), lambda b, t: (rows(b, t), COL_GA)),
            pl.BlockSpec((tt, D_MODEL), lambda b, t: (rows(b, t), COL_GB)),
            pl.BlockSpec((None, CONV_W, N_SLABS, SUBLANES, LANES), lambda b, t: (l, 0, 0, 0, 0)),
            pl.BlockSpec((None, N_SLABS, SUBLANES, LANES), lambda b, t: (l, 0, 0, 0)),
            _layer_vec(l), _layer_vec(l),
            pl.BlockSpec((None, None, CONV_BUF, D_MODEL), lambda b, t: (l, b, 0, 0)),
        ],
        out_specs=[pl.BlockSpec((tt, D_MODEL), lambda b, t: (rows(b, t), 0)),
                   pl.BlockSpec((None, CONV_BUF, D_MODEL), lambda b, t: (b, 0, 0))],
        out_shape=[
            jax.ShapeDtypeStruct((m, D_MODEL), BF16),
            jax.ShapeDtypeStruct((batch, CONV_BUF, D_MODEL), F32),
        ],
        scratch_shapes=_conv_scratch(tt),
        compiler_params=_params(("parallel", "arbitrary")),
        name="conv",
    )(p, p, wts["conv_w"], wts["conv_b"], wts["ln_g"], wts["ln_b"], state)


def _xattn_kernel(q_ref, k_ref, v_ref, y_ref, kb_ref, vb_ref):
    @pl.when(pl.program_id(1) == 0)
    def _():
        kb_ref[...] = k_ref[...].astype(BF16)
        vb_ref[...] = v_ref[...].astype(BF16)

    _xattn_body((q_ref, 0), kb_ref, vb_ref, y_ref)


def _xattn(p, mk, mv, l, *, batch, seq, tt):
    nt = seq // tt
    m = batch * seq
    kv_spec = pl.BlockSpec((None, None, MEM_LEN, D_MODEL), lambda b, t: (l, b, 0, 0))
    return pl.pallas_call(
        _xattn_kernel,
        grid=(batch, nt),
        in_specs=[pl.BlockSpec((tt, D_MODEL), lambda b, t: (b * nt + t, COL_XQ)), kv_spec, kv_spec],
        out_specs=pl.BlockSpec((tt, D_MODEL), lambda b, t: (b * nt + t, 0)),
        out_shape=jax.ShapeDtypeStruct((m, D_MODEL), BF16),
        scratch_shapes=[pltpu.VMEM((MEM_LEN, D_MODEL), BF16), pltpu.VMEM((MEM_LEN, D_MODEL), BF16)],
        compiler_params=_params(("parallel", "arbitrary")),
        name="xattn",
    )(p, mk, mv)


def _merge_kernel(x_ref, yml_ref, ycv_ref, yxa_ref, g0_ref, g1_ref, g2_ref, wb_ref, wo_ref, gp_ref, o_ref):
    mixed = None
    for i, (y_ref, gate_ref) in enumerate(((yml_ref, g0_ref), (ycv_ref, g1_ref), (yxa_ref, g2_ref))):
        term = _sigmoid(gate_ref[...].astype(F32)) * jnp.dot(y_ref[...], wb_ref[i], preferred_element_type=F32)
        mixed = term if mixed is None else mixed + term
    z = jnp.dot(mixed.astype(BF16), wo_ref[...], preferred_element_type=F32)
    o_ref[...] = x_ref[...] + _rms(z, gp_ref[...])


def _merge(x2d, y_ml, y_cv, y_xa, gates, gate_col, wts, l, *, tm):
    m = x2d.shape[0]
    tile = lambda c: pl.BlockSpec((tm, D_MODEL), lambda i: (i, c))
    return pl.pallas_call(
        _merge_kernel,
        grid=(m // tm,),
        in_specs=[
            tile(0), tile(0), tile(0), tile(0),
            tile(gate_col), tile(gate_col + 1), tile(gate_col + 2),
            _resident((None, N_BRANCH, D_MODEL, D_MODEL), lambda i: (l, 0, 0, 0)),
            _resident((None, D_MODEL, D_MODEL), lambda i: (l, 0, 0)),
            _layer_vec(l),
        ],
        out_specs=tile(0),
        out_shape=jax.ShapeDtypeStruct((m, D_MODEL), F32),
        compiler_params=_params(("parallel",)),
        name="merge",
    )(x2d, y_ml, y_cv, y_xa, gates, gates, gates, wts["w_branch"], wts["w_out"], wts["g_mix_post"])


FFN_CHUNKS = ((0, 768), (768, 768), (1536, 768), (2304, 512))


def _ffn_kernel(x_ref, gpre_ref, wa_ref, wb_ref, wo_ref, gpost_ref, o_ref, h_ref):
    x = x_ref[...]
    xn = _rms(x, gpre_ref[...]).astype(BF16)
    for start, size in FFN_CHUNKS:
        sl = slice(start, start + size)
        fa = jnp.dot(xn, wa_ref[:, sl], preferred_element_type=F32)
        fb = jnp.dot(xn, wb_ref[:, sl], preferred_element_type=F32)
        h_ref[:, sl] = (fa * _sigmoid(fa) * fb).astype(BF16)
    z = jnp.dot(h_ref[...], wo_ref[...], preferred_element_type=F32)
    o_ref[...] = x + _rms(z, gpost_ref[...])


def _ffn(x2d, wts, l, *, tm):
    m = x2d.shape[0]
    return pl.pallas_call(
        _ffn_kernel,
        grid=(m // tm,),
        in_specs=[
            pl.BlockSpec((tm, D_MODEL), lambda i: (i, 0)),
            _layer_vec(l),
            _resident((None, D_MODEL, D_FF), lambda i: (l, 0, 0)),
            _resident((None, D_MODEL, D_FF), lambda i: (l, 0, 1)),
            _resident((None, D_FF, D_MODEL), lambda i: (l, 0, 0)),
            _layer_vec(l),
        ],
        out_specs=pl.BlockSpec((tm, D_MODEL), lambda i: (i, 0)),
        out_shape=jax.ShapeDtypeStruct((m, D_MODEL), F32),
        scratch_shapes=[pltpu.VMEM((tm, D_FF), BF16)],
        compiler_params=_params(("parallel",)),
        name="ffn",
    )(x2d, wts["g_ffn_pre"], wts["w_ffn_in"], wts["w_ffn_in"], wts["w_ffn_out"], wts["g_ffn_post"])


def _memkv_kernel(x_ref, g_ref, w_ref, k_ref, v_ref):
    xn = _rms(x_ref[...], g_ref[...]).astype(BF16)
    k_ref[...] = jnp.dot(xn, w_ref[:, :D_MODEL], preferred_element_type=F32)
    v_ref[...] = jnp.dot(xn, w_ref[:, D_MODEL:], preferred_element_type=F32)


def _memkv(mem2d, wts, l, *, tm):
    m = mem2d.shape[0]
    out = jax.ShapeDtypeStruct((m, D_MODEL), F32)
    tile = pl.BlockSpec((tm, D_MODEL), lambda i: (i, 0))
    return pl.pallas_call(
        _memkv_kernel,
        grid=(m // tm,),
        in_specs=[tile, _layer_vec(l), _resident((None, D_MODEL, 2 * D_MODEL), lambda i: (l, 0, 0))],
        out_specs=[tile, tile],
        out_shape=[out, out],
        compiler_params=_params(("parallel",)),
        name="memkv",
    )(mem2d, wts["g_mem"], wts["w_mem_kv"])


def _layer_first_chunks(x2d, wts, l, mk, mv, *, batch, seq):
    tiles = _tiles(batch, seq)
    x2d, ml_state, conv_out = _branches(x2d, wts, l, mk, mv, batch=batch, seq=seq, tt=tiles["L"])
    x2d = _ffn(x2d, wts, l, tm=tiles["tm"])
    return x2d, ml_state, conv_out


def _layer_later_chunk(x2d, wts, l, ml_state, conv_state, mk, mv, *, batch, seq):
    tiles = _tiles(batch, seq)
    p, go, gto = _inproj(x2d, wts, l, batch=batch, seq=seq)
    y_ml, ml_state = _mlstm(p, go, gto, wts, l, ml_state, batch=batch, seq=seq, L=tiles["L"])
    y_cv, conv_out = _conv(p, wts, l, conv_state, batch=batch, seq=seq, tt=tiles["L"])
    y_xa = _xattn(p, mk, mv, l, batch=batch, seq=seq, tt=tiles["tt_xa"])
    x2d = _merge(x2d, y_ml, y_cv, y_xa, p, COL_GATE, wts, l, tm=tiles["tm"])
    x2d = _ffn(x2d, wts, l, tm=tiles["tm"])
    return x2d, ml_state, conv_out


def kernel(x_prompt, x_sample, mem_prompt, state_mlstm_c, state_mlstm_n, state_mlstm_m, state_conv,
           cache_mem_k, cache_mem_v, norm_mix_pre, norm_mix_post, norm_ffn_pre, norm_ffn_post, w_in, b_gate,
           mlstm_head_norm, conv_w, conv_b, conv_ln_g, conv_ln_b, mem_norm, w_mem_kv, w_branch, w_out,
           w_ffn_in, w_ffn_out):
    depth = w_in.shape[0]
    bp, sp, _ = x_prompt.shape
    bs, ss, _ = x_sample.shape
    gate_lo = 4 * D_MODEL
    gate_hi = gate_lo + N_GATES

    vec = lambda v: v.reshape(depth, 1, -1).astype(F32)
    w_gate_cols = w_in[:, :, gate_lo:gate_hi]
    wts = {
        "w_lo": w_in[:, :, :gate_lo].astype(BF16), "w_hi": w_in[:, :, gate_hi:].astype(BF16),
        "w_gate": jnp.pad(w_gate_cols, ((0, 0), (0, 0), (0, GATE_PAD - N_GATES))).astype(BF16),
        "w_gate_t": jnp.pad(jnp.swapaxes(w_gate_cols, 1, 2), ((0, 0), (0, GATE_ROWS - N_GATES), (0, 0))).astype(BF16),
        "bias_row": jnp.pad(b_gate, ((0, 0), (0, GATE_PAD - N_GATES))).reshape(depth, 1, GATE_PAD).astype(F32),
        "bias_col": jnp.pad(b_gate, ((0, 0), (0, GATE_ROWS - N_GATES))).reshape(depth, GATE_ROWS, 1).astype(F32),
        "g_mix_pre": vec(norm_mix_pre), "g_mix_post": vec(norm_mix_post),
        "g_ffn_pre": vec(norm_ffn_pre), "g_ffn_post": vec(norm_ffn_post),
        "g_mhead": vec(mlstm_head_norm), "g_mem": vec(mem_norm),
        "conv_w": jnp.broadcast_to(conv_w.astype(F32).reshape(depth, CONV_W, N_SLABS, 1, LANES),
                                   (depth, CONV_W, N_SLABS, SUBLANES, LANES)),
        "conv_b": jnp.broadcast_to(conv_b.astype(F32).reshape(depth, N_SLABS, 1, LANES),
                                   (depth, N_SLABS, SUBLANES, LANES)),
        "ln_g": vec(conv_ln_g), "ln_b": vec(conv_ln_b),
        "w_branch": w_branch.astype(BF16), "w_out": w_out.astype(BF16),
        "w_ffn_in": w_ffn_in.astype(BF16), "w_ffn_out": w_ffn_out.astype(BF16),
        "w_mem_kv": w_mem_kv.astype(BF16),
    }
    s_state = (state_mlstm_c.astype(F32), state_mlstm_n.astype(F32),
               state_mlstm_m.astype(F32).reshape(depth, bs, 1, ML_HEADS))
    s_conv = state_conv.astype(F32)
    cache_k = cache_mem_k.reshape(depth, bs, MEM_LEN, D_MODEL)
    cache_v = cache_mem_v.reshape(depth, bs, MEM_LEN, D_MODEL)

    yp = x_prompt.reshape(bp * sp, D_MODEL)
    ys = x_sample.reshape(bs * ss, D_MODEL)
    mem2d = mem_prompt.reshape(bp * MEM_LEN, D_MODEL)

    pc, pn, pm, pconv, pmk, pmv = [], [], [], [], [], []
    sc, sn, sm, sconv = [], [], [], []
    for l in range(depth):
        mk_p, mv_p = _memkv(mem2d, wts, l, tm=256)
        yp, (c1, n1, m1), buf1 = _layer_first_chunks(
            yp, wts, l, mk_p.reshape(bp, MEM_LEN, D_MODEL), mv_p.reshape(bp, MEM_LEN, D_MODEL), batch=bp, seq=sp)
        ys, (c2, n2, m2), buf2 = _layer_later_chunk(ys, wts, l, s_state, s_conv, cache_k, cache_v, batch=bs, seq=ss)
        pc.append(c1); pn.append(n1); pm.append(m1); pconv.append(buf1)
        pmk.append(mk_p.reshape(bp, MEM_LEN, XA_HEADS, XA_DH)); pmv.append(mv_p.reshape(bp, MEM_LEN, XA_HEADS, XA_DH))
        sc.append(c2); sn.append(n2); sm.append(m2); sconv.append(buf2)
    return (yp.reshape(bp, sp, D_MODEL), ys.reshape(bs, ss, D_MODEL),
            jnp.stack(pc), jnp.stack(pn), jnp.stack(pm), jnp.stack(pconv), jnp.stack(pmk), jnp.stack(pmv),
            jnp.stack(sc), jnp.stack(sn), jnp.stack(sm), jnp.stack(sconv))
```
